```python
import jax, jax.numpy as jnp
from jax import lax
import numpy as np

D_MODEL = 4096
BATCH = 1
SEQ = 8192
DEPTH = 2

SSM_GROUP = 16
SSM_WIDTH = D_MODEL // 2
SSM_GROUPS = SSM_WIDTH // SSM_GROUP
SSM_STATE = 64
GDN_HEADS = 16
GDN_HEAD_K = 128
GDN_HEAD_V = 128
GDN_KEY = GDN_HEADS * GDN_HEAD_K
GDN_VAL = GDN_HEADS * GDN_HEAD_V
CONV_WIDTH = 4
CHUNK = 64
D_FF = 4 * D_MODEL
EPS = 1e-6
IN_SPLITS = (SSM_WIDTH, GDN_KEY, GDN_KEY, GDN_VAL, GDN_HEADS, GDN_HEADS, GDN_VAL, D_MODEL, D_MODEL)
D_IN_PROJ = SSM_WIDTH + 2 * GDN_KEY + 2 * GDN_VAL + 2 * GDN_HEADS + 2 * D_MODEL
CONV_CH = 2 * GDN_KEY + GDN_VAL

kernel_name = "hybrid_s5_gated_deltanet_sandwich_block"


def rmsnorm(x, w):
    xf = x.astype(jnp.float32)
    y = xf * lax.rsqrt(jnp.mean(xf * xf, axis=-1, keepdims=True) + EPS) * w.astype(jnp.float32)
    return y.astype(x.dtype)


def l2norm(t):
    return t * lax.rsqrt(jnp.sum(t * t, axis=-1, keepdims=True) + EPS)


def s5_ssm(u, a_re, a_im, log_dt, b_re, b_im, c_re, c_im, d_skip):
    bsz, L, _ = u.shape
    uf = u.astype(jnp.float32).reshape(bsz, L, SSM_GROUPS, SSM_GROUP)
    dt = jnp.exp(log_dt.astype(jnp.float32))[:, None]
    ar = a_re.astype(jnp.float32)
    ai = a_im.astype(jnp.float32)
    mag = jnp.exp(ar * dt)
    abar_r = mag * jnp.cos(ai * dt)
    abar_i = mag * jnp.sin(ai * dt)
    den = ar * ar + ai * ai
    fr = ((abar_r - 1.0) * ar + abar_i * ai) / den
    fi = (abar_i * ar - (abar_r - 1.0) * ai) / den
    br = b_re.astype(jnp.float32)
    bi = b_im.astype(jnp.float32)
    bbar_r = fr[..., None] * br - fi[..., None] * bi
    bbar_i = fr[..., None] * bi + fi[..., None] * br
    bu_r = jnp.einsum('blgh,gnh->blgn', uf, bbar_r)
    bu_i = jnp.einsum('blgh,gnh->blgn', uf, bbar_i)

    def combine(e1, e2):
        a1r, a1i, b1r, b1i = e1
        a2r, a2i, b2r, b2i = e2
        return (a2r * a1r - a2i * a1i,
                a2r * a1i + a2i * a1r,
                a2r * b1r - a2i * b1i + b2r,
                a2r * b1i + a2i * b1r + b2i)

    ar_b = jnp.broadcast_to(abar_r, bu_r.shape)
    ai_b = jnp.broadcast_to(abar_i, bu_r.shape)
    _, _, xr, xi = lax.associative_scan(combine, (ar_b, ai_b, bu_r, bu_i), axis=1)
    y = (jnp.einsum('blgn,ghn->blgh', xr, c_re.astype(jnp.float32))
         - jnp.einsum('blgn,ghn->blgh', xi, c_im.astype(jnp.float32)))
    y = y + d_skip.astype(jnp.float32).reshape(SSM_GROUPS, SSM_GROUP) * uf
    return y.reshape(bsz, L, SSM_WIDTH).astype(u.dtype)


def causal_conv_silu(x, w):
    K = w.shape[0]
    L = x.shape[1]
    xp = jnp.pad(x, ((0, 0), (K - 1, 0), (0, 0)))
    y = xp[:, 0:L] * w[0]
    for j in range(1, K):
        y = y + xp[:, j:j + L] * w[j]
    return jax.nn.silu(y)


def chunk_gated_delta_rule(q, k, v, g, beta):
    bsz, L, H, Dk = q.shape
    Dv = v.shape[-1]
    n = L // CHUNK
    q = l2norm(q) * (Dk ** -0.5)
    k = l2norm(k)

    def chunked(t):
        return t.reshape(bsz, n, CHUNK, H, -1).transpose(0, 3, 1, 2, 4)

    q, k, v = chunked(q), chunked(k), chunked(v)
    g = g.reshape(bsz, n, CHUNK, H).transpose(0, 3, 1, 2)
    beta = beta.reshape(bsz, n, CHUNK, H).transpose(0, 3, 1, 2)
    g = jnp.cumsum(g, axis=-1)
    causal = jnp.tril(jnp.ones((CHUNK, CHUNK), dtype=bool))
    strict = jnp.tril(jnp.ones((CHUNK, CHUNK), dtype=bool), -1)
    decay = jnp.exp(jnp.where(causal, g[..., :, None] - g[..., None, :], -jnp.inf))
    kk = jnp.einsum('bhncd,bhnsd->bhncs', k * beta[..., None], k)
    a_mat = jnp.where(strict, kk * decay, 0.0) + jnp.eye(CHUNK, dtype=kk.dtype)
    u = lax.linalg.triangular_solve(a_mat, v * beta[..., None], left_side=True, lower=True, unit_diagonal=True)
    w = lax.linalg.triangular_solve(a_mat, k * (beta * jnp.exp(g))[..., None], left_side=True, lower=True, unit_diagonal=True)
    qk = jnp.where(causal, jnp.einsum('bhncd,bhnsd->bhncs', q, k) * decay, 0.0)

    def to_chunk_major(t):
        return jnp.moveaxis(t, 2, 0)

    xs = (to_chunk_major(q), to_chunk_major(k), to_chunk_major(u), to_chunk_major(w),
          to_chunk_major(qk), to_chunk_major(g))

    def step(S, inp):
        qc, kc, uc, wc, qkc, gc = inp
        v_new = uc - jnp.einsum('bhcd,bhde->bhce', wc, S)
        o = (jnp.einsum('bhcd,bhde->bhce', qc * jnp.exp(gc)[..., None], S)
             + jnp.einsum('bhcs,bhse->bhce', qkc, v_new))
        g_last = gc[..., -1]
        S = (S * jnp.exp(g_last)[..., None, None]
             + jnp.einsum('bhcd,bhce->bhde', kc * jnp.exp(g_last[..., None] - gc)[..., None], v_new))
        return S, o

    S0 = jnp.zeros((bsz, H, Dk, Dv), dtype=q.dtype)
    _, o = lax.scan(step, S0, xs)
    return o.transpose(1, 0, 3, 2, 4).reshape(bsz, L, H, Dv)


def hybrid_mixer(h, w_in, a_re, a_im, log_dt, b_re, b_im, c_re, c_im, d_skip, w_glu,
                 conv_w, a_log, dt_bias, gdn_norm_w, w_gdn_out, w_out):
    bsz, L, _ = h.shape
    proj = h @ w_in
    offsets = np.cumsum(np.array(IN_SPLITS))[:-1].tolist()
    u, q, k, v, a, b, z, gate_s, gate_d = jnp.split(proj, offsets, axis=-1)

    y = jax.nn.gelu(s5_ssm(u, a_re, a_im, log_dt, b_re, b_im, c_re, c_im, d_skip))
    glu_a, glu_b = jnp.split(y @ w_glu, 2, axis=-1)
    y_ssm = glu_a * jax.nn.sigmoid(glu_b)

    qkv = causal_conv_silu(jnp.concatenate([q, k, v], axis=-1), conv_w).astype(jnp.float32)
    qc, kc, vc = jnp.split(qkv, [GDN_KEY, 2 * GDN_KEY], axis=-1)
    qc = qc.reshape(bsz, L, GDN_HEADS, GDN_HEAD_K)
    kc = kc.reshape(bsz, L, GDN_HEADS, GDN_HEAD_K)
    vc = vc.reshape(bsz, L, GDN_HEADS, GDN_HEAD_V)
    g = -jnp.exp(a_log.astype(jnp.float32)) * jax.nn.softplus(a.astype(jnp.float32) + dt_bias.astype(jnp.float32))
    beta = jax.nn.sigmoid(b.astype(jnp.float32))
    o = chunk_gated_delta_rule(qc, kc, vc, g, beta).astype(h.dtype)
    o = rmsnorm(o, gdn_norm_w) * jax.nn.silu(z.reshape(bsz, L, GDN_HEADS, GDN_HEAD_V))
    y_gdn = o.reshape(bsz, L, GDN_VAL) @ w_gdn_out

    merged = jax.nn.sigmoid(gate_s) * y_ssm + jax.nn.sigmoid(gate_d) * y_gdn
    return merged @ w_out


def squared_relu_mlp(h, w1, w2):
    return jnp.square(jax.nn.relu(h @ w1)) @ w2


def setup_inputs(seed: int = 0) -> dict:
    key = jax.random.key(seed)
    ks = jax.random.split(key, 24)
    f32 = jnp.float32

    def nrm(k, shape, scale):
        return jax.random.normal(k, shape, f32) * scale

    def gain(k, shape):
        return 1.0 + 0.02 * jax.random.normal(k, shape, f32)

    x = jax.random.normal(ks[0], (BATCH, SEQ, D_MODEL), f32)
    w_in = nrm(ks[1], (DEPTH, D_MODEL, D_IN_PROJ), D_MODEL ** -0.5)
    ssm_a_re = -0.5 * jnp.exp(0.05 * jax.random.normal(ks[2], (DEPTH, SSM_GROUPS, SSM_STATE), f32))
    n_idx = jnp.arange(SSM_STATE, dtype=f32)
    ssm_a_im = jnp.pi * n_idx + 0.01 * jax.random.normal(ks[3], (DEPTH, SSM_GROUPS, SSM_STATE), f32)
    ssm_log_dt = jax.random.uniform(ks[4], (DEPTH, SSM_GROUPS), f32, np.log(1e-3), np.log(1e-1))
    ssm_b_re = nrm(ks[5], (DEPTH, SSM_GROUPS, SSM_STATE, SSM_GROUP), (2 * SSM_GROUP) ** -0.5)
    ssm_b_im = nrm(ks[6], (DEPTH, SSM_GROUPS, SSM_STATE, SSM_GROUP), (2 * SSM_GROUP) ** -0.5)
    ssm_c_re = nrm(ks[7], (DEPTH, SSM_GROUPS, SSM_GROUP, SSM_STATE), (2 * SSM_STATE) ** -0.5)
    ssm_c_im = nrm(ks[8], (DEPTH, SSM_GROUPS, SSM_GROUP, SSM_STATE), (2 * SSM_STATE) ** -0.5)
    ssm_d = 1.0 + 0.1 * jax.random.normal(ks[9], (DEPTH, SSM_WIDTH), f32)
    w_glu = nrm(ks[10], (DEPTH, SSM_WIDTH, 2 * D_MODEL), SSM_WIDTH ** -0.5)
    conv_w = nrm(ks[11], (DEPTH, CONV_WIDTH, CONV_CH), CONV_WIDTH ** -0.5)
    gdn_a_log = jnp.log(jax.random.uniform(ks[12], (DEPTH, GDN_HEADS), f32, 1.0, 16.0))
    dt0 = jnp.exp(jax.random.uniform(ks[13], (DEPTH, GDN_HEADS), f32, np.log(1e-3), np.log(1e-1)))
    gdn_dt_bias = dt0 + jnp.log(-jnp.expm1(-dt0))
    gdn_norm_w = gain(ks[14], (DEPTH, GDN_HEAD_V))
    w_gdn_out = nrm(ks[15], (DEPTH, GDN_VAL, D_MODEL), GDN_VAL ** -0.5)
    w_out = nrm(ks[16], (DEPTH, D_MODEL, D_MODEL), D_MODEL ** -0.5)
    mix_pre_w = gain(ks[17], (DEPTH, D_MODEL))
    mix_post_w = gain(ks[18], (DEPTH, D_MODEL))
    ffn_pre_w = gain(ks[19], (DEPTH, D_MODEL))
    ffn_post_w = gain(ks[20], (DEPTH, D_MODEL))
    w_ff1 = nrm(ks[21], (DEPTH, D_MODEL, D_FF), D_MODEL ** -0.5)
    w_ff2 = nrm(ks[22], (DEPTH, D_FF, D_MODEL), D_FF ** -0.5)
    return {"x": x, "w_in": w_in, "ssm_a_re": ssm_a_re, "ssm_a_im": ssm_a_im,
            "ssm_log_dt": ssm_log_dt, "ssm_b_re": ssm_b_re, "ssm_b_im": ssm_b_im,
            "ssm_c_re": ssm_c_re, "ssm_c_im": ssm_c_im, "ssm_d": ssm_d, "w_glu": w_glu,
            "conv_w": conv_w, "gdn_a_log": gdn_a_log, "gdn_dt_bias": gdn_dt_bias,
            "gdn_norm_w": gdn_norm_w, "w_gdn_out": w_gdn_out, "w_out": w_out,
            "mix_pre_w": mix_pre_w, "mix_post_w": mix_post_w, "ffn_pre_w": ffn_pre_w,
            "ffn_post_w": ffn_post_w, "w_ff1": w_ff1, "w_ff2": w_ff2}


def reference(x, w_in, ssm_a_re, ssm_a_im, ssm_log_dt, ssm_b_re, ssm_b_im, ssm_c_re, ssm_c_im,
              ssm_d, w_glu, conv_w, gdn_a_log, gdn_dt_bias, gdn_norm_w, w_gdn_out, w_out,
              mix_pre_w, mix_post_w, ffn_pre_w, ffn_post_w, w_ff1, w_ff2):
    for i in range(DEPTH):
        h = rmsnorm(x, mix_pre_w[i])
        h = hybrid_mixer(h, w_in[i], ssm_a_re[i], ssm_a_im[i], ssm_log_dt[i], ssm_b_re[i], ssm_b_im[i],
                         ssm_c_re[i], ssm_c_im[i], ssm_d[i], w_glu[i], conv_w[i], gdn_a_log[i],
                         gdn_dt_bias[i], gdn_norm_w[i], w_gdn_out[i], w_out[i])
        x = x + rmsnorm(h, mix_post_w[i])
        h = squared_relu_mlp(rmsnorm(x, ffn_pre_w[i]), w_ff1[i], w_ff2[i])
        x = x + rmsnorm(h, ffn_post_w[i])
    return x
```

```python
import functools
import math

import jax
import jax.numpy as jnp
import numpy as np
from jax import lax
from jax.experimental import pallas as pl
from jax.experimental.pallas import tpu as pltpu

F32 = jnp.float32
BF16 = jnp.bfloat16
EPS = 1e-6

V7X_VMEM_LIMIT_BYTES = 56 * 1024 * 1024
LANES = 128
SSM_FOLD = 8
GDN_CHUNK = 64
CONV_HALO = 8


def _cparams(sem):
    return pltpu.CompilerParams(dimension_semantics=sem, vmem_limit_bytes=V7X_VMEM_LIMIT_BYTES)


def _rms(xf, w):
    return xf * lax.rsqrt(jnp.mean(xf * xf, axis=-1, keepdims=True) + EPS) * w


def _sigmoid(x):
    return 1.0 / (1.0 + jnp.exp(-x))


def _prenorm_kernel(x_ref, w_ref, o_ref):
    o_ref[...] = _rms(x_ref[...], w_ref[...]).astype(o_ref.dtype)


def _prenorm(x, w, *, tr=256):
    L, D = x.shape
    return pl.pallas_call(
        _prenorm_kernel,
        grid=(L // tr,),
        in_specs=[pl.BlockSpec((tr, D), lambda i: (i, 0)), pl.BlockSpec((1, D), lambda i: (0, 0))],
        out_specs=pl.BlockSpec((tr, D), lambda i: (i, 0)),
        out_shape=jax.ShapeDtypeStruct((L, D), BF16),
        compiler_params=_cparams(("parallel",)),
        name="prenorm",
    )(x, w.reshape(1, D))


def _resnorm_kernel(h_ref, x_ref, wpost_ref, wpre_ref, xo_ref, ho_ref):
    xn = x_ref[...] + _rms(h_ref[...].astype(F32), wpost_ref[...])
    xo_ref[...] = xn
    ho_ref[...] = _rms(xn, wpre_ref[...]).astype(ho_ref.dtype)


def _resnorm_last_kernel(h_ref, x_ref, wpost_ref, xo_ref):
    xo_ref[...] = x_ref[...] + _rms(h_ref[...].astype(F32), wpost_ref[...])


def _resnorm(h, x, w_post, w_pre, *, tr=256):
    L, D = x.shape
    row = pl.BlockSpec((tr, D), lambda i: (i, 0))
    vec = pl.BlockSpec((1, D), lambda i: (0, 0))
    if w_pre is None:
        return pl.pallas_call(
            _resnorm_last_kernel, grid=(L // tr,), in_specs=[row, row, vec], out_specs=row,
            out_shape=jax.ShapeDtypeStruct((L, D), F32),
            compiler_params=_cparams(("parallel",)), name="resnorm_last",
        )(h, x, w_post.reshape(1, D)), None
    return pl.pallas_call(
        _resnorm_kernel, grid=(L // tr,), in_specs=[row, row, vec, vec], out_specs=[row, row],
        out_shape=[jax.ShapeDtypeStruct((L, D), F32), jax.ShapeDtypeStruct((L, D), BF16)],
        compiler_params=_cparams(("parallel",)), name="resnorm",
    )(h, x, w_post.reshape(1, D), w_pre.reshape(1, D))


def _mm_kernel(*refs, n_b, n_epi, nk, epilogue):
    a_ref = refs[0]
    b_refs = refs[1:1 + n_b]
    epi_refs = refs[1 + n_b:1 + n_b + n_epi]
    o_ref = refs[1 + n_b + n_epi]
    acc_refs = refs[2 + n_b + n_epi:]
    a = a_ref[...].astype(BF16)

    def finish(accs):
        o_ref[...] = epilogue(*accs, *[r[...] for r in epi_refs]).astype(o_ref.dtype)

    if nk == 1:
        finish([jnp.dot(a, b[...], preferred_element_type=F32) for b in b_refs])
        return
    k = pl.program_id(2)

    @pl.when(k == 0)
    def _():
        for b, acc in zip(b_refs, acc_refs):
            acc[...] = jnp.dot(a, b[...], preferred_element_type=F32)

    @pl.when(k > 0)
    def _():
        for b, acc in zip(b_refs, acc_refs):
            acc[...] += jnp.dot(a, b[...], preferred_element_type=F32)

    @pl.when(k == nk - 1)
    def _():
        finish([acc[...] for acc in acc_refs])


def _matmul(a, bs, *, n, out_dtype, tm, tn, tk, name, epilogue=None, epi=()):
    M, K = a.shape
    nk = K // tk
    if epilogue is None:
        epilogue = lambda acc: acc
    in_specs = [pl.BlockSpec((tm, tk), lambda i, j, k: (i, k))]
    args = [a]
    for b, off in bs:
        in_specs.append(pl.BlockSpec((tk, tn), lambda i, j, k, off=off: (k, off + j)))
        args.append(b)
    for e, off in epi:
        in_specs.append(pl.BlockSpec((tm, tn), lambda i, j, k, off=off: (i, off + j)))
        args.append(e)
    scratch = [pltpu.VMEM((tm, tn), F32) for _ in bs] if nk > 1 else []
    kern = functools.partial(_mm_kernel, n_b=len(bs), n_epi=len(epi), nk=nk, epilogue=epilogue)
    return pl.pallas_call(
        kern,
        grid=(M // tm, n // tn, nk),
        in_specs=in_specs,
        out_specs=pl.BlockSpec((tm, tn), lambda i, j, k: (i, j)),
        out_shape=jax.ShapeDtypeStruct((M, n), out_dtype),
        scratch_shapes=scratch,
        compiler_params=_cparams(("parallel", "parallel", "arbitrary")),
        name=name,
    )(*args)


def _glu_epilogue(acc_a, acc_b):
    return acc_a * _sigmoid(acc_b)


def _merge_epilogue(acc, gate_s, gate_d, y_ssm):
    return (_sigmoid(gate_s.astype(F32)) * y_ssm.astype(F32)
            + _sigmoid(gate_d.astype(F32)) * acc)


def _relu2_epilogue(acc):
    r = jnp.maximum(acc, 0.0)
    return r * r


def _ssm_tables(a_re, a_im, log_dt, b_re, b_im, c_re, c_im, d_skip):
    T = SSM_FOLD
    G, N = a_re.shape
    H = b_re.shape[-1]
    gps = LANES // H
    n_slab = G // gps
    dt = jnp.exp(log_dt.astype(F32))[:, None]
    ar = a_re.astype(F32)
    ai = a_im.astype(F32)
    mag = jnp.exp(ar * dt)
    abar_r = mag * jnp.cos(ai * dt)
    abar_i = mag * jnp.sin(ai * dt)
    den = ar * ar + ai * ai
    fr = ((abar_r - 1.0) * ar + abar_i * ai) / den
    fi = (abar_i * ar - (abar_r - 1.0) * ai) / den
    br = b_re.astype(F32)
    bi = b_im.astype(F32)
    bbar_r = fr[..., None] * br - fi[..., None] * bi
    bbar_i = fr[..., None] * bi + fi[..., None] * br
    p = jnp.arange(T + 1, dtype=F32)[:, None, None]
    pmag = jnp.exp(p * (ar * dt)[None])
    pow_r = pmag * jnp.cos(p * (ai * dt)[None])
    pow_i = pmag * jnp.sin(p * (ai * dt)[None])
    cr = c_re.astype(F32)
    ci = c_im.astype(F32)
    eye = jnp.eye(gps, dtype=F32)

    pw_r = pow_r[T - 1 - jnp.arange(T)]
    pw_i = pow_i[T - 1 - jnp.arange(T)]
    pb_r = pw_r[..., None] * bbar_r[None] - pw_i[..., None] * bbar_i[None]
    pb_i = pw_r[..., None] * bbar_i[None] + pw_i[..., None] * bbar_r[None]
    pc = jnp.stack([pb_r, pb_i], axis=0)
    pc = pc.reshape(2, T, n_slab, gps, N, H)
    ptab = jnp.einsum("cisgnh,gk->sighckn", pc, eye)
    ptab = ptab.reshape(n_slab, T * gps * H, 2 * gps * N)

    qw_r = pow_r[1:T + 1]
    qw_i = pow_i[1:T + 1]
    ca_r = cr[None] * qw_r[:, :, None, :] - ci[None] * qw_i[:, :, None, :]
    ca_i = cr[None] * qw_i[:, :, None, :] + ci[None] * qw_r[:, :, None, :]
    qc = jnp.stack([ca_r, -ca_i], axis=0)
    qc = qc.reshape(2, T, n_slab, gps, H, N)
    qtab = jnp.einsum("cjsghn,gk->scknjgh", qc, eye)
    qtab = qtab.reshape(n_slab, 2 * gps * N, T * gps * H)

    cb_r = (jnp.einsum("ghn,lgn,gnk->lghk", cr, pow_r[:T], bbar_r)
            - jnp.einsum("ghn,lgn,gnk->lghk", cr, pow_i[:T], bbar_i)
            - jnp.einsum("ghn,lgn,gnk->lghk", ci, pow_r[:T], bbar_i)
            - jnp.einsum("ghn,lgn,gnk->lghk", ci, pow_i[:T], bbar_r))
    ii = jnp.arange(T)[:, None]
    jj = jnp.arange(T)[None, :]
    lag = jnp.clip(jj - ii, 0, T - 1)
    kij = jnp.where((jj >= ii)[:, :, None, None, None], cb_r[lag], 0.0)
    kij = kij.reshape(T, T, n_slab, gps, H, H)
    mtab = jnp.einsum("ijsghk,gm->sigkjmh", kij, eye)
    mtab = mtab.reshape(n_slab, T * gps * H, T * gps * H)

    a_pow = jnp.stack([pow_r[T].reshape(n_slab, gps * N), pow_i[T].reshape(n_slab, gps * N)], axis=1)
    d_tab = jnp.tile(d_skip.astype(F32).reshape(n_slab, 1, LANES), (1, 1, T))
    return ptab.astype(BF16), mtab.astype(BF16), qtab.astype(BF16), a_pow, d_tab


def _ssm_kernel(u_ref, p_ref, m_ref, q_ref, apow_ref, d_ref, o_ref, s_ref, xp_ref):
    T = SSM_FOLD
    nc = u_ref.shape[0] // T
    half = s_ref.shape[1] // 2
    uf32 = jnp.concatenate([u_ref[pl.ds(i, nc, stride=T), :] for i in range(T)], axis=1)
    uf = uf32.astype(BF16)
    s_ref[...] = jnp.dot(uf, p_ref[0], preferred_element_type=F32)
    ar = apow_ref[0, 0:1, :]
    ai = apow_ref[0, 1:2, :]

    def step(c, carry):
        xr, xi = carry
        xp_ref[pl.ds(c, 1), 0:half] = xr
        xp_ref[pl.ds(c, 1), half:2 * half] = xi
        sr = s_ref[pl.ds(c, 1), 0:half]
        si = s_ref[pl.ds(c, 1), half:2 * half]
        return ar * xr - ai * xi + sr, ar * xi + ai * xr + si

    zero = jnp.zeros((1, half), F32)
    lax.fori_loop(0, nc, step, (zero, zero), unroll=8)
    y = (jnp.dot(uf, m_ref[0], preferred_element_type=F32)
         + jnp.dot(xp_ref[...].astype(BF16), q_ref[0], preferred_element_type=F32)
         + d_ref[0] * uf32)
    g = jax.nn.gelu(y, approximate=True)
    for j in range(T):
        o_ref[pl.ds(j, nc, stride=T), :] = g[:, j * LANES:(j + 1) * LANES]


def _ssm(u, tables):
    ptab, mtab, qtab, a_pow, d_tab = tables
    L, W = u.shape
    n_slab = W // LANES
    nc = L // SSM_FOLD
    fw = SSM_FOLD * LANES
    ns = ptab.shape[2]
    tab = lambda r, c: pl.BlockSpec((1, r, c), lambda s: (s, 0, 0))
    return pl.pallas_call(
        _ssm_kernel,
        grid=(n_slab,),
        in_specs=[pl.BlockSpec((L, LANES), lambda s: (0, s)),
                  tab(fw, ns), tab(fw, fw), tab(ns, fw), tab(2, ns // 2), tab(1, fw)],
        out_specs=pl.BlockSpec((L, LANES), lambda s: (0, s)),
        out_shape=jax.ShapeDtypeStruct((L, W), F32),
        scratch_shapes=[pltpu.VMEM((nc, ns), F32), pltpu.VMEM((nc, ns), F32)],
        compiler_params=_cparams(("parallel",)),
        name="s5_ssm",
    )(u, ptab, mtab, qtab, a_pow, d_tab)


def _dot_nt(a, b):
    return lax.dot_general(a, b, (((1,), (1,)), ((), ())), preferred_element_type=F32)


def _dot_tn(a, b):
    return lax.dot_general(a, b, (((0,), (0,)), ((), ())), preferred_element_type=F32)


def _gdn_kernel(qkv_ref, ab_ref, z_ref, convw_ref, alog_ref, dtb_ref, normw_ref,
                o_ref, ext_ref, s_ref, *, n_heads, dk, dv):
    C = GDN_CHUNK
    HALO = CONV_HALO
    kw = convw_ref.shape[0]
    c = pl.program_id(0)

    @pl.when(c == 0)
    def _():
        ext_ref[0:HALO, :] = jnp.zeros((HALO, ext_ref.shape[1]), F32)
        s_ref[...] = jnp.zeros(s_ref.shape, F32)

    @pl.when(c > 0)
    def _():
        ext_ref[0:HALO, :] = ext_ref[C:C + HALO, :]

    ext_ref[HALO:HALO + C, :] = qkv_ref[...].astype(F32)

    row = lax.broadcasted_iota(jnp.int32, (C, 2 * C), 0)
    col = lax.broadcasted_iota(jnp.int32, (C, 2 * C), 1)
    colm = jnp.where(col >= C, col - C, col)
    causal2 = colm <= row
    strict2 = colm < row
    low_half = col < C
    eye2 = jnp.where(colm == row, 1.0, 0.0)
    tri = jnp.where(lax.broadcasted_iota(jnp.int32, (C, C), 1) <= lax.broadcasted_iota(jnp.int32, (C, C), 0),
                    1.0, 0.0)

    a = ab_ref[:, 0:LANES]
    b = ab_ref[:, LANES:2 * LANES]
    sp_in = a + dtb_ref[...]
    softplus = jnp.maximum(sp_in, 0.0) + jnp.log(1.0 + jnp.exp(-jnp.abs(sp_in)))
    g = -jnp.exp(alog_ref[...]) * softplus
    beta = _sigmoid(b)
    gc = jnp.dot(tri, g, preferred_element_type=F32, precision=lax.Precision.HIGHEST)
    gct = jnp.transpose(jnp.concatenate([gc, gc], axis=0))

    def conv_silu(col0):
        y = ext_ref[pl.ds(HALO - (kw - 1), C), pl.ds(col0, dk)] * convw_ref[0:1, pl.ds(col0, dk)]
        for j in range(1, kw):
            y = y + ext_ref[pl.ds(HALO - (kw - 1) + j, C), pl.ds(col0, dk)] * convw_ref[j:j + 1, pl.ds(col0, dk)]
        return y * _sigmoid(y)

    zero_cc = jnp.zeros((C, 2 * C), BF16)
    for h in range(n_heads):
        qh = conv_silu(h * dk)
        kh = conv_silu(n_heads * dk + h * dk)
        vh = conv_silu(2 * n_heads * dk + h * dv)
        qn = qh * lax.rsqrt(jnp.sum(qh * qh, axis=-1, keepdims=True) + EPS) * (dk ** -0.5)
        kn = kh * lax.rsqrt(jnp.sum(kh * kh, axis=-1, keepdims=True) + EPS)
        gcol = gc[:, h:h + 1]
        grow2 = gct[h:h + 1, :]
        bcol = beta[:, h:h + 1]
        glast = gc[C - 1:C, h:h + 1]
        decay2 = jnp.exp(jnp.where(causal2, gcol - grow2, -jnp.inf))
        kb = kn * bcol
        knb = kn.astype(BF16)
        k2 = jnp.concatenate([knb, knb], axis=0)
        kk2 = _dot_nt(kb.astype(BF16), k2)
        n2 = jnp.where(strict2, kk2 * decay2, 0.0)
        nb = jnp.where(low_half, 0.0, n2).astype(BF16)
        sq = jnp.dot(nb, jnp.concatenate([zero_cc, n2.astype(BF16)], axis=0), preferred_element_type=F32)
        w = jnp.where(low_half, eye2 - n2, sq)
        for _ in range(int(math.log2(C)) - 1):
            wb = w.astype(BF16)
            out = jnp.dot(wb, jnp.concatenate([zero_cc, wb], axis=0), preferred_element_type=F32)
            w = jnp.where(low_half, w, 0.0) + out
        tb = jnp.where(low_half, w, 0.0).astype(BF16)
        r = jnp.concatenate([vh * bcol, kb * jnp.exp(gcol)], axis=1).astype(BF16)
        uw = jnp.dot(tb, jnp.concatenate([r, r], axis=0), preferred_element_type=F32)
        u = uw[:, 0:dv]
        wk = uw[:, dv:dv + dk]
        qk2 = jnp.where(causal2 & low_half, _dot_nt(qn.astype(BF16), k2) * decay2, 0.0).astype(BF16)
        s = s_ref[h]
        sb = s.astype(BF16)
        v_new = u - jnp.dot(wk.astype(BF16), sb, preferred_element_type=F32)
        vb = v_new.astype(BF16)
        o = (jnp.dot((qn * jnp.exp(gcol)).astype(BF16), sb, preferred_element_type=F32)
             + jnp.dot(qk2, jnp.concatenate([vb, vb], axis=0), preferred_element_type=F32))
        kd = (kn * jnp.exp(glast - gcol)).astype(BF16)
        s_ref[h] = s * jnp.exp(glast) + _dot_tn(kd, vb)
        zh = z_ref[:, h * dv:(h + 1) * dv].astype(F32)
        on = _rms(o, normw_ref[...]) * (zh * _sigmoid(zh))
        o_ref[:, h * dv:(h + 1) * dv] = on.astype(o_ref.dtype)


def _gdn(qkv, ab, zg, conv_w, a_log, dt_bias, norm_w, *, n_heads, dk, dv):
    L, cw = qkv.shape
    C = GDN_CHUNK
    pad = lambda v: jnp.pad(v.astype(F32), (0, LANES - v.shape[0])).reshape(1, LANES)
    kern = functools.partial(_gdn_kernel, n_heads=n_heads, dk=dk, dv=dv)
    return pl.pallas_call(
        kern,
        grid=(L // C,),
        in_specs=[pl.BlockSpec((C, cw), lambda c: (c, 0)),
                  pl.BlockSpec((C, 2 * LANES), lambda c: (c, 0)),
                  pl.BlockSpec((C, n_heads * dv), lambda c: (c, 0)),
                  pl.BlockSpec(conv_w.shape, lambda c: (0, 0)),
                  pl.BlockSpec((1, LANES), lambda c: (0, 0)),
                  pl.BlockSpec((1, LANES), lambda c: (0, 0)),
                  pl.BlockSpec((1, dv), lambda c: (0, 0))],
        out_specs=pl.BlockSpec((C, n_heads * dv), lambda c: (c, 0)),
        out_shape=jax.ShapeDtypeStruct((L, n_heads * dv), BF16),
        scratch_shapes=[pltpu.VMEM((C + CONV_HALO, cw), F32), pltpu.VMEM((n_heads, dk, dv), F32)],
        compiler_params=_cparams(("arbitrary",)),
        name="gated_deltanet",
    )(qkv, ab, zg, conv_w.astype(F32), pad(a_log), pad(dt_bias), norm_w.astype(F32).reshape(1, dv))


def kernel(x, w_in, ssm_a_re, ssm_a_im, ssm_log_dt, ssm_b_re, ssm_b_im, ssm_c_re, ssm_c_im, ssm_d, w_glu, conv_w, gdn_a_log, gdn_dt_bias, gdn_norm_w, w_gdn_out, w_out, mix_pre_w, mix_post_w, ffn_pre_w, ffn_post_w, w_ff1, w_ff2):
    depth = w_in.shape[0]
    B, L, D = x.shape
    assert B == 1
    n_heads = gdn_a_log.shape[1]
    dv = gdn_norm_w.shape[1]
    gdn_val = w_gdn_out.shape[1]
    assert gdn_val == n_heads * dv
    conv_ch = conv_w.shape[2]
    gdn_key = (conv_ch - gdn_val) // 2
    dk = gdn_key // n_heads
    ssm_w = ssm_d.shape[1]
    d_ff = w_ff1.shape[2]
    o_u = 0
    o_qkv = o_u + ssm_w
    o_a = o_qkv + conv_ch
    o_b = o_a + n_heads
    o_z = o_b + n_heads
    o_gs = o_z + gdn_val
    o_gd = o_gs + D
    assert o_gd + D == w_in.shape[2]
    TM = TN = 1024

    xs = x.reshape(L, D)
    h = _prenorm(xs, mix_pre_w[0])
    for i in range(depth):
        wi = w_in[i]
        w_u = wi[:, o_u:o_qkv].astype(BF16)
        w_qkv = wi[:, o_qkv:o_a].astype(BF16)
        w_ab = jnp.zeros((D, 2 * LANES), BF16)
        w_ab = w_ab.at[:, 0:n_heads].set(wi[:, o_a:o_b].astype(BF16))
        w_ab = w_ab.at[:, LANES:LANES + n_heads].set(wi[:, o_b:o_z].astype(BF16))
        w_zg = wi[:, o_z:].astype(BF16)

        u = _matmul(h, [(w_u, 0)], n=ssm_w, out_dtype=F32, tm=TM, tn=TN, tk=D, name="in_proj_u")
        qkv = _matmul(h, [(w_qkv, 0)], n=conv_ch, out_dtype=BF16, tm=TM, tn=TN, tk=D, name="in_proj_qkv")
        ab = _matmul(h, [(w_ab, 0)], n=2 * LANES, out_dtype=F32, tm=TM, tn=2 * LANES, tk=D, name="in_proj_ab")
        zg = _matmul(h, [(w_zg, 0)], n=gdn_val + 2 * D, out_dtype=BF16, tm=TM, tn=TN, tk=D, name="in_proj_zg")

        tables = _ssm_tables(ssm_a_re[i], ssm_a_im[i], ssm_log_dt[i], ssm_b_re[i], ssm_b_im[i],
                             ssm_c_re[i], ssm_c_im[i], ssm_d[i])
        y = _ssm(u, tables)
        wg = w_glu[i].astype(BF16)
        y_ssm = _matmul(y, [(wg, 0), (wg, D // TN)], n=D, out_dtype=BF16, tm=TM, tn=TN, tk=ssm_w,
                        name="glu", epilogue=_glu_epilogue)

        o = _gdn(qkv, ab, zg, conv_w[i], gdn_a_log[i], gdn_dt_bias[i], gdn_norm_w[i],
                 n_heads=n_heads, dk=dk, dv=dv)
        merged = _matmul(o, [(w_gdn_out[i].astype(BF16), 0)], n=D, out_dtype=BF16, tm=TM, tn=TN, tk=gdn_val,
                         name="gdn_out_merge", epilogue=_merge_epilogue,
                         epi=[(zg, gdn_val // TN), (zg, (gdn_val + D) // TN), (y_ssm, 0)])
        hm = _matmul(merged, [(w_out[i].astype(BF16), 0)], n=D, out_dtype=BF16, tm=TM, tn=TN, tk=D, name="out_proj")
        xs, hn = _resnorm(hm, xs, mix_post_w[i], ffn_pre_w[i])

        f1 = _matmul(hn, [(w_ff1[i].astype(BF16), 0)], n=d_ff, out_dtype=BF16, tm=TM, tn=TN, tk=D,
                     name="ff1", epilogue=_relu2_epilogue)
        f2 = _matmul(f1, [(w_ff2[i].astype(BF16), 0)], n=D, out_dtype=BF16, tm=TM, tn=TN, tk=2048, name="ff2")
        xs, h = _resnorm(f2, xs, ffn_post_w[i], mix_pre_w[i + 1] if i + 1 < depth else None)
    return xs.reshape(B, L, D)
```

```python
import functools
import math

import jax
import jax.numpy as jnp
import numpy as np
from jax import lax
from jax.experimental import pallas as pl
from jax.experimental.pallas import tpu as pltpu

F32 = jnp.float32
BF16 = jnp.bfloat16
EPS = 1e-6

V7X_VMEM_LIMIT_BYTES = 56 * 1024 * 1024
LANES = 128
SSM_FOLD = 8
GDN_CHUNK = 64
CONV_HALO = 8
MM_TM = 1024
MM_TN = 512
MM_TK_FF2 = 2048


def _cparams(sem):
    return pltpu.CompilerParams(dimension_semantics=sem, vmem_limit_bytes=V7X_VMEM_LIMIT_BYTES)


def _rms(xf, w):
    return xf * lax.rsqrt(jnp.mean(xf * xf, axis=-1, keepdims=True) + EPS) * w


def _sigmoid(x):
    return 1.0 / (1.0 + jnp.exp(-x))


def _prenorm_kernel(x_ref, w_ref, o_ref):
    o_ref[...] = _rms(x_ref[...], w_ref[...]).astype(o_ref.dtype)


def _prenorm(x, w, *, tr=256):
    L, D = x.shape
    return pl.pallas_call(
        _prenorm_kernel,
        grid=(L // tr,),
        in_specs=[pl.BlockSpec((tr, D), lambda i: (i, 0)), pl.BlockSpec((1, D), lambda i: (0, 0))],
        out_specs=pl.BlockSpec((tr, D), lambda i: (i, 0)),
        out_shape=jax.ShapeDtypeStruct((L, D), BF16),
        compiler_params=_cparams(("parallel",)),
        name="prenorm",
    )(x, w.reshape(1, D))


def _resnorm_kernel(h_ref, x_ref, wpost_ref, wpre_ref, xo_ref, ho_ref):
    xn = x_ref[...] + _rms(h_ref[...].astype(F32), wpost_ref[...])
    xo_ref[...] = xn
    ho_ref[...] = _rms(xn, wpre_ref[...]).astype(ho_ref.dtype)


def _resnorm_last_kernel(h_ref, x_ref, wpost_ref, xo_ref):
    xo_ref[...] = x_ref[...] + _rms(h_ref[...].astype(F32), wpost_ref[...])


def _resnorm(h, x, w_post, w_pre, *, tr=256):
    L, D = x.shape
    row = pl.BlockSpec((tr, D), lambda i: (i, 0))
    vec = pl.BlockSpec((1, D), lambda i: (0, 0))
    if w_pre is None:
        return pl.pallas_call(
            _resnorm_last_kernel, grid=(L // tr,), in_specs=[row, row, vec], out_specs=row,
            out_shape=jax.ShapeDtypeStruct((L, D), F32),
            compiler_params=_cparams(("parallel",)), name="resnorm_last",
        )(h, x, w_post.reshape(1, D)), None
    return pl.pallas_call(
        _resnorm_kernel, grid=(L // tr,), in_specs=[row, row, vec, vec], out_specs=[row, row],
        out_shape=[jax.ShapeDtypeStruct((L, D), F32), jax.ShapeDtypeStruct((L, D), BF16)],
        compiler_params=_cparams(("parallel",)), name="resnorm",
    )(h, x, w_post.reshape(1, D), w_pre.reshape(1, D))


def _mm_kernel(*refs, n_b, n_epi, nk, tm, stage, epilogue):
    a_ref = refs[0]
    w_refs = refs[1:1 + n_b]
    epi_refs = refs[1 + n_b:1 + n_b + n_epi]
    o_ref = refs[1 + n_b + n_epi]
    scratch = refs[2 + n_b + n_epi:]
    k = pl.program_id(1)
    i = pl.program_id(2)
    if stage:
        wb_refs = scratch[:n_b]
        scratch = scratch[n_b:]

        @pl.when(i == 0)
        def _():
            for w, wb in zip(w_refs, wb_refs):
                wb[...] = w[...].astype(BF16)
    else:
        wb_refs = w_refs
    acc_refs = scratch
    a = a_ref[...].astype(BF16)
    parts = [jnp.dot(a, wb[...], preferred_element_type=F32) for wb in wb_refs]

    def finish(accs):
        o_ref[...] = epilogue(*accs, *[r[...] for r in epi_refs]).astype(o_ref.dtype)

    if nk == 1:
        finish(parts)
        return
    rows = pl.ds(pl.multiple_of(i * tm, tm), tm)

    @pl.when(k == 0)
    def _():
        for acc, part in zip(acc_refs, parts):
            acc[rows, :] = part

    @pl.when(jnp.logical_and(k > 0, k < nk - 1))
    def _():
        for acc, part in zip(acc_refs, parts):
            acc[rows, :] += part

    @pl.when(k == nk - 1)
    def _():
        finish([acc[rows, :] + part for acc, part in zip(acc_refs, parts)])


def _matmul(a, ws, *, n, out_dtype, tm, tn, tk, name, epilogue=None, epi=()):
    M, K = a.shape
    nk = K // tk
    assert M % tm == 0 and n % tn == 0 and K % tk == 0
    if epilogue is None:
        epilogue = lambda acc: acc
    stage = any(w.dtype != BF16 for w, _, _ in ws)
    in_specs = [pl.BlockSpec((tm, tk), lambda j, k, i: (i, k))]
    args = [a]
    for w, layer, off in ws:
        if layer is None:
            in_specs.append(pl.BlockSpec((tk, tn), lambda j, k, i, off=off: (k, off + j)))
        else:
            in_specs.append(pl.BlockSpec((None, tk, tn), lambda j, k, i, layer=layer, off=off: (layer, k, off + j)))
        args.append(w)
    for e, off in epi:
        in_specs.append(pl.BlockSpec((tm, tn), lambda j, k, i, off=off: (i, off + j)))
        args.append(e)
    scratch = [pltpu.VMEM((tk, tn), BF16) for _ in ws] if stage else []
    if nk > 1:
        scratch += [pltpu.VMEM((M, tn), F32) for _ in ws]
        out_map = lambda j, k, i: (jnp.where(k == nk - 1, i, 0), j)
    else:
        out_map = lambda j, k, i: (i, j)
    kern = functools.partial(_mm_kernel, n_b=len(ws), n_epi=len(epi), nk=nk, tm=tm, stage=stage,
                             epilogue=epilogue)
    return pl.pallas_call(
        kern,
        grid=(n // tn, nk, M // tm),
        in_specs=in_specs,
        out_specs=pl.BlockSpec((tm, tn), out_map),
        out_shape=jax.ShapeDtypeStruct((M, n), out_dtype),
        scratch_shapes=scratch,
        compiler_params=_cparams(("arbitrary", "arbitrary", "arbitrary")),
        name=name,
    )(*args)


def _glu_epilogue(acc_a, acc_b):
    return acc_a * _sigmoid(acc_b)


def _merge_epilogue(acc, gate_s, gate_d, y_ssm):
    return (_sigmoid(gate_s.astype(F32)) * y_ssm.astype(F32)
            + _sigmoid(gate_d.astype(F32)) * acc)


def _relu2_epilogue(acc):
    r = jnp.maximum(acc, 0.0)
    return r * r


def _dot_nt(a, b):
    return lax.dot_general(a, b, (((1,), (1,)), ((), ())), preferred_element_type=F32)


def _dot_tn(a, b):
    return lax.dot_general(a, b, (((0,), (0,)), ((), ())), preferred_element_type=F32)


def _ssm_tables(a_re, a_im, log_dt, b_re, b_im, c_re, c_im, d_skip):
    T = SSM_FOLD
    G, N = a_re.shape
    H = b_re.shape[-1]
    gps = LANES // H
    n_slab = G // gps
    dt = jnp.exp(log_dt.astype(F32))[:, None]
    ar = a_re.astype(F32)
    ai = a_im.astype(F32)
    mag = jnp.exp(ar * dt)
    abar_r = mag * jnp.cos(ai * dt)
    abar_i = mag * jnp.sin(ai * dt)
    den = ar * ar + ai * ai
    fr = ((abar_r - 1.0) * ar + abar_i * ai) / den
    fi = (abar_i * ar - (abar_r - 1.0) * ai) / den
    br = b_re.astype(F32)
    bi = b_im.astype(F32)
    bbar_r = fr[..., None] * br - fi[..., None] * bi
    bbar_i = fr[..., None] * bi + fi[..., None] * br
    p = jnp.arange(T + 1, dtype=F32)[:, None, None]
    pmag = jnp.exp(p * (ar * dt)[None])
    pow_r = pmag * jnp.cos(p * (ai * dt)[None])
    pow_i = pmag * jnp.sin(p * (ai * dt)[None])
    cr = c_re.astype(F32)
    ci = c_im.astype(F32)
    rows = T * gps * H

    def compact(t):
        t = t.reshape(2, T, n_slab, gps, H, N).transpose(2, 0, 1, 3, 4, 5).reshape(n_slab, 2, rows, N)
        return jnp.concatenate([t, t], axis=-1).astype(BF16)

    pw_r = pow_r[T - 1 - jnp.arange(T)]
    pw_i = pow_i[T - 1 - jnp.arange(T)]
    pb_r = pw_r[..., None] * bbar_r[None] - pw_i[..., None] * bbar_i[None]
    pb_i = pw_r[..., None] * bbar_i[None] + pw_i[..., None] * bbar_r[None]
    pc = compact(jnp.stack([pb_r, pb_i], axis=0).transpose(0, 1, 2, 4, 3))

    qw_r = pow_r[1:T + 1]
    qw_i = pow_i[1:T + 1]
    ca_r = cr[None] * qw_r[:, :, None, :] - ci[None] * qw_i[:, :, None, :]
    ca_i = cr[None] * qw_i[:, :, None, :] + ci[None] * qw_r[:, :, None, :]
    qc = compact(jnp.stack([ca_r, -ca_i], axis=0))

    cb_r = (jnp.einsum("ghn,lgn,gnk->lghk", cr, pow_r[:T], bbar_r)
            - jnp.einsum("ghn,lgn,gnk->lghk", cr, pow_i[:T], bbar_i)
            - jnp.einsum("ghn,lgn,gnk->lghk", ci, pow_r[:T], bbar_i)
            - jnp.einsum("ghn,lgn,gnk->lghk", ci, pow_i[:T], bbar_r))
    ii = jnp.arange(T)[:, None]
    jj = jnp.arange(T)[None, :]
    lag = jnp.clip(jj - ii, 0, T - 1)
    kij = jnp.where((jj >= ii)[:, :, None, None, None], cb_r[lag], 0.0)
    kc = kij.reshape(T, T, n_slab, gps, H, H).transpose(2, 0, 3, 5, 1, 4)
    kc = kc.reshape(n_slab, rows, T * H).astype(BF16)
    src = np.arange(T * H)
    dst = np.arange(rows)
    spread = ((src[:, None] // H == dst[None, :] // (gps * H)) & (src[:, None] % H == dst[None, :] % H))
    spread = jnp.asarray(spread, BF16)

    a_pow = jnp.stack([pow_r[T].reshape(n_slab, gps * N), pow_i[T].reshape(n_slab, gps * N)], axis=1)
    d_tab = jnp.tile(d_skip.astype(F32).reshape(n_slab, 1, LANES), (1, 1, T))
    return pc, qc, kc, spread, a_pow, d_tab


def _ssm_kernel(u_ref, p_ref, q_ref, k_ref, spread_ref, apow_ref, d_ref, o_ref, s_ref, xp_ref, *, group):
    T = SSM_FOLD
    nc = u_ref.shape[0] // T
    ns = s_ref.shape[1]
    half = ns // 2
    fw = T * LANES
    n_state = p_ref.shape[3] // 2
    gps = LANES // group
    row_g = (lax.broadcasted_iota(jnp.int32, (fw, LANES), 0) // group) % gps
    lane = lax.broadcasted_iota(jnp.int32, (fw, LANES), 1)

    def state_table(ref):
        tiles = []
        for c in range(2):
            t = ref[0, c]
            for v in range(half // LANES):
                tiles.append(jnp.where(row_g == v * (LANES // n_state) + lane // n_state, t, jnp.zeros_like(t)))
        return jnp.concatenate(tiles, axis=1)

    p_tab = state_table(p_ref)
    qt_tab = state_table(q_ref)
    m_rep = jnp.dot(k_ref[0], spread_ref[...], preferred_element_type=F32)
    same_g = ((lax.broadcasted_iota(jnp.int32, (fw, fw), 0) // group) % gps
              == (lax.broadcasted_iota(jnp.int32, (fw, fw), 1) // group) % gps)
    m_tab = jnp.where(same_g, m_rep, 0.0).astype(BF16)
    uf32 = jnp.concatenate([u_ref[pl.ds(i, nc, stride=T), :] for i in range(T)], axis=1)
    uf = uf32.astype(BF16)
    s_ref[...] = jnp.dot(uf, p_tab, preferred_element_type=F32)
    ar = apow_ref[0, 0:1, :]
    ai = apow_ref[0, 1:2, :]

    def step(c, carry):
        xr, xi = carry
        xp_ref[pl.ds(c, 1), 0:half] = xr
        xp_ref[pl.ds(c, 1), half:2 * half] = xi
        sr = s_ref[pl.ds(c, 1), 0:half]
        si = s_ref[pl.ds(c, 1), half:2 * half]
        return ar * xr - ai * xi + sr, ar * xi + ai * xr + si

    zero = jnp.zeros((1, half), F32)
    lax.fori_loop(0, nc, step, (zero, zero), unroll=8)
    y = (jnp.dot(uf, m_tab, preferred_element_type=F32)
         + _dot_nt(xp_ref[...].astype(BF16), qt_tab)
         + d_ref[0] * uf32)
    g = jax.nn.gelu(y, approximate=True)
    for j in range(T):
        o_ref[pl.ds(j, nc, stride=T), :] = g[:, j * LANES:(j + 1) * LANES]


def _ssm(u, tables):
    pc, qc, kc, spread, a_pow, d_tab = tables
    L, W = u.shape
    n_slab = W // LANES
    nc = L // SSM_FOLD
    fw = SSM_FOLD * LANES
    ns = 2 * a_pow.shape[2]
    group = LANES * pc.shape[3] // ns
    tab = lambda r, c: pl.BlockSpec((1, r, c), lambda s: (s, 0, 0))
    tab2 = pl.BlockSpec((1, 2, fw, pc.shape[3]), lambda s: (s, 0, 0, 0))
    return pl.pallas_call(
        functools.partial(_ssm_kernel, group=group),
        grid=(n_slab,),
        in_specs=[pl.BlockSpec((L, LANES), lambda s: (0, s)),
                  tab2, tab2, tab(fw, kc.shape[2]), pl.BlockSpec(spread.shape, lambda s: (0, 0)),
                  tab(2, ns // 2), tab(1, fw)],
        out_specs=pl.BlockSpec((L, LANES), lambda s: (0, s)),
        out_shape=jax.ShapeDtypeStruct((L, W), F32),
        scratch_shapes=[pltpu.VMEM((nc, ns), F32), pltpu.VMEM((nc, ns), F32)],
        compiler_params=_cparams(("parallel",)),
        name="s5_ssm",
    )(u, pc, qc, kc, spread, a_pow, d_tab)


def _gdn_kernel(qkv_ref, ab_ref, z_ref, convw_ref, alog_ref, dtb_ref, normw_ref,
                o_ref, ext_ref, s_ref, *, n_heads, dk, dv):
    C = GDN_CHUNK
    HALO = CONV_HALO
    kw = convw_ref.shape[0]
    c = pl.program_id(0)

    @pl.when(c == 0)
    def _():
        ext_ref[0:HALO, :] = jnp.zeros((HALO, ext_ref.shape[1]), F32)
        s_ref[...] = jnp.zeros(s_ref.shape, F32)

    @pl.when(c > 0)
    def _():
        ext_ref[0:HALO, :] = ext_ref[C:C + HALO, :]

    ext_ref[HALO:HALO + C, :] = qkv_ref[...].astype(F32)

    row = lax.broadcasted_iota(jnp.int32, (C, 2 * C), 0)
    col = lax.broadcasted_iota(jnp.int32, (C, 2 * C), 1)
    colm = jnp.where(col >= C, col - C, col)
    causal2 = colm <= row
    strict2 = colm < row
    low_half = col < C
    eye2 = jnp.where(colm == row, 1.0, 0.0)
    tri = jnp.where(lax.broadcasted_iota(jnp.int32, (C, C), 1) <= lax.broadcasted_iota(jnp.int32, (C, C), 0),
                    1.0, 0.0)

    ab = ab_ref[...]
    sp_in = ab + dtb_ref[...]
    softplus = jnp.maximum(sp_in, 0.0) + jnp.log(1.0 + jnp.exp(-jnp.abs(sp_in)))
    g = -jnp.exp(alog_ref[...]) * softplus
    beta = _sigmoid(ab)
    gc = jnp.dot(tri, g, preferred_element_type=F32, precision=lax.Precision.HIGHEST)
    gct = jnp.transpose(jnp.concatenate([gc, gc], axis=0))

    def conv_silu(col0):
        y = ext_ref[pl.ds(HALO - (kw - 1), C), pl.ds(col0, dk)] * convw_ref[0:1, pl.ds(col0, dk)]
        for j in range(1, kw):
            y = y + ext_ref[pl.ds(HALO - (kw - 1) + j, C), pl.ds(col0, dk)] * convw_ref[j:j + 1, pl.ds(col0, dk)]
        return y * _sigmoid(y)

    heads = range(n_heads)
    zero_cc = jnp.zeros((C, 2 * C), F32)
    dot = functools.partial(jnp.dot, preferred_element_type=F32)
    st = []
    for h in heads:
        qh = conv_silu(h * dk)
        kh = conv_silu(n_heads * dk + h * dk)
        vh = conv_silu(2 * n_heads * dk + h * dv)
        qn = qh * lax.rsqrt(jnp.sum(qh * qh, axis=-1, keepdims=True) + EPS) * (dk ** -0.5)
        kn = kh * lax.rsqrt(jnp.sum(kh * kh, axis=-1, keepdims=True) + EPS)
        gcol = gc[:, h:h + 1]
        grow2 = gct[h:h + 1, :]
        bcol = beta[:, n_heads + h:n_heads + h + 1]
        glast = gc[C - 1:C, h:h + 1]
        egc = jnp.exp(gcol)
        kb = kn * bcol
        r = jnp.concatenate([vh * bcol, kb * egc], axis=1)
        st.append(dict(
            decay2=jnp.exp(jnp.where(causal2, gcol - grow2, -jnp.inf)),
            k2=jnp.concatenate([kn, kn], axis=0).astype(BF16),
            kbq=jnp.concatenate([kb, qn], axis=0).astype(BF16),
            r2=jnp.concatenate([r, jnp.zeros_like(r)], axis=0).astype(BF16),
            qg=qn * egc,
            kd=(kn * jnp.exp(glast - gcol)).astype(BF16),
            eg=jnp.exp(glast)))
    top = lambda x: jnp.concatenate([x, zero_cc], axis=0).astype(BF16)
    sc = [_dot_nt(t["kbq"], t["k2"]) for t in st]
    n2 = [jnp.where(strict2, s_[0:C] * t["decay2"], 0.0) for s_, t in zip(sc, st)]
    qk2 = [jnp.where(causal2 & low_half, s_[C:2 * C] * t["decay2"], 0.0).astype(BF16) for s_, t in zip(sc, st)]
    valid = low_half & strict2
    blk = 2
    w = [jnp.where(low_half, eye2, 0.0) - jnp.where(valid & (row // blk == colm // blk), n, 0.0) for n in n2]
    while blk < C:
        off = valid & (row // (2 * blk) == colm // (2 * blk)) & (row // blk > colm // blk)
        y = [dot(jnp.where(off, n, 0.0).astype(BF16), top(x)) for n, x in zip(n2, w)]
        w = [x - dot(x.astype(BF16), top(y_)) for x, y_ in zip(w, y)]
        blk *= 2
    uw = [dot(x.astype(BF16), t["r2"]) for x, t in zip(w, st)]
    s_old = [s_ref[h] for h in heads]
    ws = [dot(jnp.concatenate([x[:, dv:dv + dk], t["qg"]], axis=0).astype(BF16), s_.astype(BF16))
          for x, t, s_ in zip(uw, st, s_old)]
    vn = [x[:, 0:dv] - y[0:C] for x, y in zip(uw, ws)]
    o = [y[C:2 * C] + dot(q_, jnp.concatenate([v_, v_], axis=0).astype(BF16)) for y, q_, v_ in zip(ws, qk2, vn)]
    s_new = [s_ * t["eg"] + _dot_tn(t["kd"], v_.astype(BF16)) for s_, t, v_ in zip(s_old, st, vn)]
    for h in heads:
        s_ref[h] = s_new[h]
        zh = z_ref[:, h * dv:(h + 1) * dv].astype(F32)
        on = _rms(o[h], normw_ref[...]) * (zh * _sigmoid(zh))
        o_ref[:, h * dv:(h + 1) * dv] = on.astype(o_ref.dtype)


def _gdn(qkv, ab, zg, conv_w, a_log, dt_bias, norm_w, *, n_heads, dk, dv):
    L, cw = qkv.shape
    C = GDN_CHUNK
    pad = lambda v: jnp.pad(v.astype(F32), (0, LANES - v.shape[0])).reshape(1, LANES)
    kern = functools.partial(_gdn_kernel, n_heads=n_heads, dk=dk, dv=dv)
    return pl.pallas_call(
        kern,
        grid=(L // C,),
        in_specs=[pl.BlockSpec((C, cw), lambda c: (c, 0)),
                  pl.BlockSpec((C, LANES), lambda c: (c, 0)),
                  pl.BlockSpec((C, n_heads * dv), lambda c: (c, 0)),
                  pl.BlockSpec(conv_w.shape, lambda c: (0, 0)),
                  pl.BlockSpec((1, LANES), lambda c: (0, 0)),
                  pl.BlockSpec((1, LANES), lambda c: (0, 0)),
                  pl.BlockSpec((1, dv), lambda c: (0, 0))],
        out_specs=pl.BlockSpec((C, n_heads * dv), lambda c: (c, 0)),
        out_shape=jax.ShapeDtypeStruct((L, n_heads * dv), BF16),
        scratch_shapes=[pltpu.VMEM((C + CONV_HALO, cw), F32), pltpu.VMEM((n_heads, dk, dv), F32)],
        compiler_params=_cparams(("arbitrary",)),
        name="gated_deltanet",
    )(qkv, ab, zg, conv_w.astype(F32), pad(a_log), pad(dt_bias), norm_w.astype(F32).reshape(1, dv))


def kernel(x, w_in, ssm_a_re, ssm_a_im, ssm_log_dt, ssm_b_re, ssm_b_im, ssm_c_re, ssm_c_im, ssm_d, w_glu, conv_w, gdn_a_log, gdn_dt_bias, gdn_norm_w, w_gdn_out, w_out, mix_pre_w, mix_post_w, ffn_pre_w, ffn_post_w, w_ff1, w_ff2):
    depth = w_in.shape[0]
    B, L, D = x.shape
    assert B == 1
    n_heads = gdn_a_log.shape[1]
    dv = gdn_norm_w.shape[1]
    gdn_val = w_gdn_out.shape[1]
    assert gdn_val == n_heads * dv
    conv_ch = conv_w.shape[2]
    gdn_key = (conv_ch - gdn_val) // 2
    dk = gdn_key // n_heads
    ssm_w = ssm_d.shape[1]
    d_ff = w_ff1.shape[2]
    o_u = 0
    o_qkv = o_u + ssm_w
    o_a = o_qkv + conv_ch
    o_b = o_a + n_heads
    o_z = o_b + n_heads
    o_gs = o_z + gdn_val
    o_gd = o_gs + D
    assert o_gd + D == w_in.shape[2]
    assert o_qkv % MM_TN == 0 and o_a % LANES == 0 and 2 * n_heads <= LANES
    mm = functools.partial(_matmul, tm=min(MM_TM, L), tn=MM_TN)

    xs = x.reshape(L, D)
    h = _prenorm(xs, mix_pre_w[0])
    for i in range(depth):
        w_zg = w_in[i, :, o_z:].astype(BF16)

        u = mm(h, [(w_in, i, o_u // MM_TN)], n=ssm_w, out_dtype=F32, tk=D, name="in_proj_u")
        qkv = mm(h, [(w_in, i, o_qkv // MM_TN)], n=conv_ch, out_dtype=BF16, tk=D, name="in_proj_qkv")
        ab = _matmul(h, [(w_in, i, o_a // LANES)], n=LANES, out_dtype=F32, tm=min(MM_TM, L), tn=LANES, tk=D,
                     name="in_proj_ab")
        zg = mm(h, [(w_zg, None, 0)], n=gdn_val + 2 * D, out_dtype=BF16, tk=D, name="in_proj_zg")

        tables = _ssm_tables(ssm_a_re[i], ssm_a_im[i], ssm_log_dt[i], ssm_b_re[i], ssm_b_im[i],
                             ssm_c_re[i], ssm_c_im[i], ssm_d[i])
        y = _ssm(u, tables)
        y_ssm = mm(y, [(w_glu, i, 0), (w_glu, i, D // MM_TN)], n=D, out_dtype=BF16, tk=ssm_w,
                   name="glu", epilogue=_glu_epilogue)

        o = _gdn(qkv, ab, zg, conv_w[i], gdn_a_log[i], gdn_dt_bias[i], gdn_norm_w[i],
                 n_heads=n_heads, dk=dk, dv=dv)
        merged = mm(o, [(w_gdn_out, i, 0)], n=D, out_dtype=BF16, tk=gdn_val,
                    name="gdn_out_merge", epilogue=_merge_epilogue,
                    epi=[(zg, gdn_val // MM_TN), (zg, (gdn_val + D) // MM_TN), (y_ssm, 0)])
        hm = mm(merged, [(w_out, i, 0)], n=D, out_dtype=BF16, tk=D, name="out_proj")
        xs, hn = _resnorm(hm, xs, mix_post_w[i], ffn_pre_w[i])

        f1 = mm(hn, [(w_ff1, i, 0)], n=d_ff, out_dtype=BF16, tk=D, name="ff1", epilogue=_relu2_epilogue)
        f2 = mm(f1, [(w_ff2, i, 0)], n=D, out_dtype=BF16, tk=MM_TK_FF2, name="ff2")
        xs, h = _resnorm(f2, xs, ffn_post_w[i], mix_pre_w[i + 1] if i + 1 < depth else None)
    return xs.reshape(B, L, D)
```

```python
import functools
import math

import jax
import jax.numpy as jnp
import numpy as np
from jax import lax
from jax.experimental import pallas as pl
from jax.experimental.pallas import tpu as pltpu

F32 = jnp.float32
BF16 = jnp.bfloat16
EPS = 1e-6

V7X_VMEM_LIMIT_BYTES = 56 * 1024 * 1024
LANES = 128
SSM_FOLD = 8
GDN_CHUNK = 64
CONV_HALO = 8
MM_TM = 1024
MM_TN = 1024
MM_TN_FF2 = 512
MM_TK_FF2 = 4096


def _cparams(sem):
    return pltpu.CompilerParams(dimension_semantics=sem, vmem_limit_bytes=V7X_VMEM_LIMIT_BYTES)


def _rms(xf, w):
    return xf * lax.rsqrt(jnp.mean(xf * xf, axis=-1, keepdims=True) + EPS) * w


def _sigmoid(x):
    return 1.0 / (1.0 + jnp.exp(-x))


def _prenorm_kernel(x_ref, w_ref, o_ref):
    o_ref[...] = _rms(x_ref[...], w_ref[...]).astype(o_ref.dtype)


def _prenorm(x, w, *, tr=256):
    L, D = x.shape
    return pl.pallas_call(
        _prenorm_kernel,
        grid=(L // tr,),
        in_specs=[pl.BlockSpec((tr, D), lambda i: (i, 0)), pl.BlockSpec((1, D), lambda i: (0, 0))],
        out_specs=pl.BlockSpec((tr, D), lambda i: (i, 0)),
        out_shape=jax.ShapeDtypeStruct((L, D), BF16),
        compiler_params=_cparams(("parallel",)),
        name="prenorm",
    )(x, w.reshape(1, D))


def _resnorm_kernel(h_ref, x_ref, wpost_ref, wpre_ref, xo_ref, ho_ref):
    xn = x_ref[...] + _rms(h_ref[...].astype(F32), wpost_ref[...])
    xo_ref[...] = xn
    ho_ref[...] = _rms(xn, wpre_ref[...]).astype(ho_ref.dtype)


def _resnorm_last_kernel(h_ref, x_ref, wpost_ref, xo_ref):
    xo_ref[...] = x_ref[...] + _rms(h_ref[...].astype(F32), wpost_ref[...])


def _resnorm(h, x, w_post, w_pre, *, tr=256):
    L, D = x.shape
    row = pl.BlockSpec((tr, D), lambda i: (i, 0))
    vec = pl.BlockSpec((1, D), lambda i: (0, 0))
    if w_pre is None:
        return pl.pallas_call(
            _resnorm_last_kernel, grid=(L // tr,), in_specs=[row, row, vec], out_specs=row,
            out_shape=jax.ShapeDtypeStruct((L, D), F32),
            compiler_params=_cparams(("parallel",)), name="resnorm_last",
        )(h, x, w_post.reshape(1, D)), None
    return pl.pallas_call(
        _resnorm_kernel, grid=(L // tr,), in_specs=[row, row, vec, vec], out_specs=[row, row],
        out_shape=[jax.ShapeDtypeStruct((L, D), F32), jax.ShapeDtypeStruct((L, D), BF16)],
        compiler_params=_cparams(("parallel",)), name="resnorm",
    )(h, x, w_post.reshape(1, D), w_pre.reshape(1, D))


def _mm_kernel(*refs, n_w, n_epi, nk, tm, tn, tk, rc, nt, w_index, epilogue):
    a_ref = refs[0]
    w_hbm = refs[1:1 + n_w]
    epi_refs = refs[1 + n_w:1 + n_w + n_epi]
    o_ref = refs[1 + n_w + n_epi]
    scratch = refs[2 + n_w + n_epi:]
    wb_refs = scratch[:n_w]
    stage_refs = scratch[n_w:2 * n_w]
    sem = scratch[2 * n_w]
    acc_refs = scratch[2 * n_w + 1:]
    j = pl.program_id(0)
    k = pl.program_id(1)
    i = pl.program_id(2)
    b = j * nk + k
    slot = lax.rem(b, 2)
    n_blocks = pl.num_programs(0) * nk

    def chunk_copy(w, blk, c, sslot):
        jj, kk = (blk, 0) if nk == 1 else (blk // nk, lax.rem(blk, nk))
        layer, off = w_index[w]
        if nt:
            src = w_hbm[w].at[layer, pl.ds(off + jj * tn + c * rc, rc), pl.ds(kk * tk, tk)]
        else:
            src = w_hbm[w].at[layer, pl.ds(kk * tk + c * rc, rc), pl.ds(off + jj * tn, tn)]
        return pltpu.make_async_copy(src, stage_refs[w].at[sslot], sem.at[w, sslot])

    def round_chunk(w, c, sslot, dslot):
        wb_refs[w][dslot, pl.ds(pl.multiple_of(c * rc, rc), rc), :] = stage_refs[w][sslot].astype(BF16)

    @pl.when(jnp.logical_and(b == 0, i == 0))
    def _():
        steps = (tn if nt else tk) // rc
        for w in range(n_w):
            chunk_copy(w, 0, 0, 0).start()
            for c in range(steps):
                if c + 1 < steps:
                    chunk_copy(w, 0, c + 1, (c + 1) % 2).start()
                chunk_copy(w, 0, c, c % 2).wait()
                round_chunk(w, c, c % 2, 0)

    has_next = b + 1 < n_blocks

    @pl.when(has_next)
    def _():
        for w in range(n_w):
            chunk_copy(w, b + 1, i, 0).start()

    a = a_ref[...].astype(BF16)
    if nt:
        parts = [_dot_nt(a, wb[slot]) for wb in wb_refs]
    else:
        parts = [jnp.dot(a, wb[slot], preferred_element_type=F32) for wb in wb_refs]

    @pl.when(has_next)
    def _():
        for w in range(n_w):
            chunk_copy(w, b + 1, i, 0).wait()
            round_chunk(w, i, 0, 1 - slot)

    def finish(accs):
        o_ref[...] = epilogue(*accs, *[r[...] for r in epi_refs]).astype(o_ref.dtype)

    if nk == 1:
        finish(parts)
        return
    rows = pl.ds(pl.multiple_of(i * tm, tm), tm)

    @pl.when(k == 0)
    def _():
        for acc, part in zip(acc_refs, parts):
            acc[rows, :] = part

    @pl.when(jnp.logical_and(k > 0, k < nk - 1))
    def _():
        for acc, part in zip(acc_refs, parts):
            acc[rows, :] += part

    @pl.when(k == nk - 1)
    def _():
        finish([acc[rows, :] + part for acc, part in zip(acc_refs, parts)])


def _matmul(a, ws, *, n, out_dtype, tm, tn, tk, name, nt=False, epilogue=None, epi=()):
    M, K = a.shape
    nk = K // tk
    n_i = M // tm
    assert M % tm == 0 and n % tn == 0 and K % tk == 0
    rc = (tn if nt else tk) // n_i
    assert rc * n_i == (tn if nt else tk) and rc % 16 == 0
    if epilogue is None:
        epilogue = lambda acc: acc
    in_specs = [pl.BlockSpec((tm, tk), lambda j, k, i: (i, k))]
    args = [a]
    for w, _, _ in ws:
        in_specs.append(pl.BlockSpec(memory_space=pl.ANY))
        args.append(w)
    for e, off in epi:
        in_specs.append(pl.BlockSpec((tm, tn), lambda j, k, i, off=off: (i, off + j)))
        args.append(e)
    wshape = (2, tn, tk) if nt else (2, tk, tn)
    scratch = [pltpu.VMEM(wshape, BF16) for _ in ws]
    scratch += [pltpu.VMEM((2, rc, wshape[2]), F32) for _ in ws]
    scratch += [pltpu.SemaphoreType.DMA((len(ws), 2))]
    if nk > 1:
        scratch += [pltpu.VMEM((M, tn), F32) for _ in ws]
        out_map = lambda j, k, i: (jnp.where(k == nk - 1, i, 0), j)
    else:
        out_map = lambda j, k, i: (i, j)
    kern = functools.partial(_mm_kernel, n_w=len(ws), n_epi=len(epi), nk=nk, tm=tm, tn=tn, tk=tk, rc=rc, nt=nt,
                             w_index=tuple((layer, off) for _, layer, off in ws), epilogue=epilogue)
    return pl.pallas_call(
        kern,
        grid=(n // tn, nk, M // tm),
        in_specs=in_specs,
        out_specs=pl.BlockSpec((tm, tn), out_map),
        out_shape=jax.ShapeDtypeStruct((M, n), out_dtype),
        scratch_shapes=scratch,
        compiler_params=_cparams(("arbitrary", "arbitrary", "arbitrary")),
        name=name,
    )(*args)


def _glu_epilogue(acc_a, acc_b):
    return acc_a * _sigmoid(acc_b)


def _merge_epilogue(acc, gate_s, gate_d, y_ssm):
    return (_sigmoid(gate_s.astype(F32)) * y_ssm.astype(F32)
            + _sigmoid(gate_d.astype(F32)) * acc)


def _relu2_epilogue(acc):
    r = jnp.maximum(acc, 0.0)
    return r * r


def _dot_nt(a, b):
    return lax.dot_general(a, b, (((1,), (1,)), ((), ())), preferred_element_type=F32)


def _dot_tn(a, b):
    return lax.dot_general(a, b, (((0,), (0,)), ((), ())), preferred_element_type=F32)


def _ssm_tables(a_re, a_im, log_dt, b_re, b_im, c_re, c_im, d_skip):
    T = SSM_FOLD
    G, N = a_re.shape
    H = b_re.shape[-1]
    gps = LANES // H
    n_slab = G // gps
    dt = jnp.exp(log_dt.astype(F32))[:, None]
    ar = a_re.astype(F32)
    ai = a_im.astype(F32)
    mag = jnp.exp(ar * dt)
    abar_r = mag * jnp.cos(ai * dt)
    abar_i = mag * jnp.sin(ai * dt)
    den = ar * ar + ai * ai
    fr = ((abar_r - 1.0) * ar + abar_i * ai) / den
    fi = (abar_i * ar - (abar_r - 1.0) * ai) / den
    br = b_re.astype(F32)
    bi = b_im.astype(F32)
    bbar_r = fr[..., None] * br - fi[..., None] * bi
    bbar_i = fr[..., None] * bi + fi[..., None] * br
    p = jnp.arange(T + 1, dtype=F32)[:, None, None]
    pmag = jnp.exp(p * (ar * dt)[None])
    pow_r = pmag * jnp.cos(p * (ai * dt)[None])
    pow_i = pmag * jnp.sin(p * (ai * dt)[None])
    cr = c_re.astype(F32)
    ci = c_im.astype(F32)
    rows = T * gps * H

    def compact(t):
        t = t.reshape(2, T, n_slab, gps, H, N).transpose(2, 0, 1, 3, 4, 5).reshape(n_slab, 2, rows, N)
        return jnp.concatenate([t, t], axis=-1).astype(BF16)

    pw_r = pow_r[T - 1 - jnp.arange(T)]
    pw_i = pow_i[T - 1 - jnp.arange(T)]
    pb_r = pw_r[..., None] * bbar_r[None] - pw_i[..., None] * bbar_i[None]
    pb_i = pw_r[..., None] * bbar_i[None] + pw_i[..., None] * bbar_r[None]
    pc = compact(jnp.stack([pb_r, pb_i], axis=0).transpose(0, 1, 2, 4, 3))

    qw_r = pow_r[1:T + 1]
    qw_i = pow_i[1:T + 1]
    ca_r = cr[None] * qw_r[:, :, None, :] - ci[None] * qw_i[:, :, None, :]
    ca_i = cr[None] * qw_i[:, :, None, :] + ci[None] * qw_r[:, :, None, :]
    qc = compact(jnp.stack([ca_r, -ca_i], axis=0))

    cb_r = (jnp.einsum("ghn,lgn,gnk->lghk", cr, pow_r[:T], bbar_r)
            - jnp.einsum("ghn,lgn,gnk->lghk", cr, pow_i[:T], bbar_i)
            - jnp.einsum("ghn,lgn,gnk->lghk", ci, pow_r[:T], bbar_i)
            - jnp.einsum("ghn,lgn,gnk->lghk", ci, pow_i[:T], bbar_r))
    ii = jnp.arange(T)[:, None]
    jj = jnp.arange(T)[None, :]
    lag = jnp.clip(jj - ii, 0, T - 1)
    kij = jnp.where((jj >= ii)[:, :, None, None, None], cb_r[lag], 0.0)
    kc = kij.reshape(T, T, n_slab, gps, H, H).transpose(2, 0, 3, 5, 1, 4)
    kc = kc.reshape(n_slab, rows, T * H).astype(BF16)
    src = np.arange(T * H)
    dst = np.arange(rows)
    spread = ((src[:, None] // H == dst[None, :] // (gps * H)) & (src[:, None] % H == dst[None, :] % H))
    spread = jnp.asarray(spread, BF16)

    a_pow = jnp.stack([pow_r[T].reshape(n_slab, gps * N), pow_i[T].reshape(n_slab, gps * N)], axis=1)
    d_tab = jnp.tile(d_skip.astype(F32).reshape(n_slab, 1, LANES), (1, 1, T))
    return pc, qc, kc, spread, a_pow, d_tab


def _ssm_kernel(u_ref, p_ref, q_ref, k_ref, spread_ref, apow_ref, d_ref, o_ref, s_ref, xp_ref, *, group):
    T = SSM_FOLD
    nc = u_ref.shape[0] // T
    ns = s_ref.shape[1]
    half = ns // 2
    fw = T * LANES
    n_state = p_ref.shape[3] // 2
    gps = LANES // group
    row_g = (lax.broadcasted_iota(jnp.int32, (fw, LANES), 0) // group) % gps
    lane = lax.broadcasted_iota(jnp.int32, (fw, LANES), 1)

    def state_table(ref):
        tiles = []
        for c in range(2):
            t = ref[0, c]
            for v in range(half // LANES):
                tiles.append(jnp.where(row_g == v * (LANES // n_state) + lane // n_state, t, jnp.zeros_like(t)))
        return jnp.concatenate(tiles, axis=1)

    p_tab = state_table(p_ref)
    qt_tab = state_table(q_ref)
    m_rep = jnp.dot(k_ref[0], spread_ref[...], preferred_element_type=F32)
    same_g = ((lax.broadcasted_iota(jnp.int32, (fw, fw), 0) // group) % gps
              == (lax.broadcasted_iota(jnp.int32, (fw, fw), 1) // group) % gps)
    m_tab = jnp.where(same_g, m_rep, 0.0).astype(BF16)
    uf32 = jnp.concatenate([u_ref[pl.ds(i, nc, stride=T), :] for i in range(T)], axis=1)
    uf = uf32.astype(BF16)
    s_ref[...] = jnp.dot(uf, p_tab, preferred_element_type=F32)
    ar = apow_ref[0, 0:1, :]
    ai = apow_ref[0, 1:2, :]

    def step(c, carry):
        xr, xi = carry
        xp_ref[pl.ds(c, 1), 0:half] = xr
        xp_ref[pl.ds(c, 1), half:2 * half] = xi
        sr = s_ref[pl.ds(c, 1), 0:half]
        si = s_ref[pl.ds(c, 1), half:2 * half]
        return ar * xr - ai * xi + sr, ar * xi + ai * xr + si

    zero = jnp.zeros((1, half), F32)
    lax.fori_loop(0, nc, step, (zero, zero), unroll=8)
    y = (jnp.dot(uf, m_tab, preferred_element_type=F32)
         + _dot_nt(xp_ref[...].astype(BF16), qt_tab)
         + d_ref[0] * uf32)
    g = jax.nn.gelu(y, approximate=True)
    for j in range(T):
        o_ref[pl.ds(j, nc, stride=T), :] = g[:, j * LANES:(j + 1) * LANES]


def _ssm(u, tables):
    pc, qc, kc, spread, a_pow, d_tab = tables
    L, W = u.shape
    n_slab = W // LANES
    nc = L // SSM_FOLD
    fw = SSM_FOLD * LANES
    ns = 2 * a_pow.shape[2]
    group = LANES * pc.shape[3] // ns
    tab = lambda r, c: pl.BlockSpec((1, r, c), lambda s: (s, 0, 0))
    tab2 = pl.BlockSpec((1, 2, fw, pc.shape[3]), lambda s: (s, 0, 0, 0))
    return pl.pallas_call(
        functools.partial(_ssm_kernel, group=group),
        grid=(n_slab,),
        in_specs=[pl.BlockSpec((L, LANES), lambda s: (0, s)),
                  tab2, tab2, tab(fw, kc.shape[2]), pl.BlockSpec(spread.shape, lambda s: (0, 0)),
                  tab(2, ns // 2), tab(1, fw)],
        out_specs=pl.BlockSpec((L, LANES), lambda s: (0, s)),
        out_shape=jax.ShapeDtypeStruct((L, W), F32),
        scratch_shapes=[pltpu.VMEM((nc, ns), F32), pltpu.VMEM((nc, ns), F32)],
        compiler_params=_cparams(("parallel",)),
        name="s5_ssm",
    )(u, pc, qc, kc, spread, a_pow, d_tab)


def _gdn_kernel(qkv_ref, ab_ref, z_ref, convw_ref, alog_ref, dtb_ref, normw_ref,
                o_ref, ext_ref, s_ref, *, n_heads, dk, dv):
    C = GDN_CHUNK
    HALO = CONV_HALO
    kw = convw_ref.shape[0]
    c = pl.program_id(0)

    @pl.when(c == 0)
    def _():
        ext_ref[0:HALO, :] = jnp.zeros((HALO, ext_ref.shape[1]), F32)
        s_ref[...] = jnp.zeros(s_ref.shape, F32)

    @pl.when(c > 0)
    def _():
        ext_ref[0:HALO, :] = ext_ref[C:C + HALO, :]

    ext_ref[HALO:HALO + C, :] = qkv_ref[...].astype(F32)

    row = lax.broadcasted_iota(jnp.int32, (C, 2 * C), 0)
    col = lax.broadcasted_iota(jnp.int32, (C, 2 * C), 1)
    colm = jnp.where(col >= C, col - C, col)
    causal2 = colm <= row
    strict2 = colm < row
    low_half = col < C
    eye2 = jnp.where(colm == row, 1.0, 0.0)
    tri = jnp.where(lax.broadcasted_iota(jnp.int32, (C, C), 1) <= lax.broadcasted_iota(jnp.int32, (C, C), 0),
                    1.0, 0.0)

    ab = ab_ref[...]
    sp_in = ab + dtb_ref[...]
    softplus = jnp.maximum(sp_in, 0.0) + jnp.log(1.0 + jnp.exp(-jnp.abs(sp_in)))
    g = -jnp.exp(alog_ref[...]) * softplus
    beta = _sigmoid(ab)
    gc = jnp.dot(tri, g, preferred_element_type=F32, precision=lax.Precision.HIGHEST)
    gct = jnp.transpose(jnp.concatenate([gc, gc], axis=0))

    def conv_silu(col0):
        y = ext_ref[pl.ds(HALO - (kw - 1), C), pl.ds(col0, dk)] * convw_ref[0:1, pl.ds(col0, dk)]
        for j in range(1, kw):
            y = y + ext_ref[pl.ds(HALO - (kw - 1) + j, C), pl.ds(col0, dk)] * convw_ref[j:j + 1, pl.ds(col0, dk)]
        return y * _sigmoid(y)

    heads = range(n_heads)
    zero_cc = jnp.zeros((C, 2 * C), F32)
    dot = functools.partial(jnp.dot, preferred_element_type=F32)
    st = []
    for h in heads:
        qh = conv_silu(h * dk)
        kh = conv_silu(n_heads * dk + h * dk)
        vh = conv_silu(2 * n_heads * dk + h * dv)
        qn = qh * lax.rsqrt(jnp.sum(qh * qh, axis=-1, keepdims=True) + EPS) * (dk ** -0.5)
        kn = kh * lax.rsqrt(jnp.sum(kh * kh, axis=-1, keepdims=True) + EPS)
        gcol = gc[:, h:h + 1]
        grow2 = gct[h:h + 1, :]
        bcol = beta[:, n_heads + h:n_heads + h + 1]
        glast = gc[C - 1:C, h:h + 1]
        egc = jnp.exp(gcol)
        kb = kn * bcol
        r = jnp.concatenate([vh * bcol, kb * egc], axis=1)
        st.append(dict(
            decay2=jnp.exp(jnp.where(causal2, gcol - grow2, -jnp.inf)),
            k2=jnp.concatenate([kn, kn], axis=0).astype(BF16),
            kbq=jnp.concatenate([kb, qn], axis=0).astype(BF16),
            r2=jnp.concatenate([r, jnp.zeros_like(r)], axis=0).astype(BF16),
            qg=qn * egc,
            kd=(kn * jnp.exp(glast - gcol)).astype(BF16),
            eg=jnp.exp(glast)))
    top = lambda x: jnp.concatenate([x, zero_cc], axis=0).astype(BF16)
    sc = [_dot_nt(t["kbq"], t["k2"]) for t in st]
    n2 = [jnp.where(strict2, s_[0:C] * t["decay2"], 0.0) for s_, t in zip(sc, st)]
    qk2 = [jnp.where(causal2 & low_half, s_[C:2 * C] * t["decay2"], 0.0).astype(BF16) for s_, t in zip(sc, st)]
    valid = low_half & strict2
    blk = 2
    w = [jnp.where(low_half, eye2, 0.0) - jnp.where(valid & (row // blk == colm // blk), n, 0.0) for n in n2]
    while blk < C:
        off = valid & (row // (2 * blk) == colm // (2 * blk)) & (row // blk > colm // blk)
        y = [dot(jnp.where(off, n, 0.0).astype(BF16), top(x)) for n, x in zip(n2, w)]
        w = [x - dot(x.astype(BF16), top(y_)) for x, y_ in zip(w, y)]
        blk *= 2
    uw = [dot(x.astype(BF16), t["r2"]) for x, t in zip(w, st)]
    s_old = [s_ref[h] for h in heads]
    ws = [dot(jnp.concatenate([x[:, dv:dv + dk], t["qg"]], axis=0).astype(BF16), s_.astype(BF16))
          for x, t, s_ in zip(uw, st, s_old)]
    vn = [x[:, 0:dv] - y[0:C] for x, y in zip(uw, ws)]
    o = [y[C:2 * C] + dot(q_, jnp.concatenate([v_, v_], axis=0).astype(BF16)) for y, q_, v_ in zip(ws, qk2, vn)]
    s_new = [s_ * t["eg"] + _dot_tn(t["kd"], v_.astype(BF16)) for s_, t, v_ in zip(s_old, st, vn)]
    for h in heads:
        s_ref[h] = s_new[h]
        zh = z_ref[:, h * dv:(h + 1) * dv].astype(F32)
        on = _rms(o[h], normw_ref[...]) * (zh * _sigmoid(zh))
        o_ref[:, h * dv:(h + 1) * dv] = on.astype(o_ref.dtype)


def _gdn(qkv, ab, zg, conv_w, a_log, dt_bias, norm_w, *, n_heads, dk, dv):
    L, cw = qkv.shape
    C = GDN_CHUNK
    pad = lambda v: jnp.pad(v.astype(F32), (0, LANES - v.shape[0])).reshape(1, LANES)
    kern = functools.partial(_gdn_kernel, n_heads=n_heads, dk=dk, dv=dv)
    return pl.pallas_call(
        kern,
        grid=(L // C,),
        in_specs=[pl.BlockSpec((C, cw), lambda c: (c, 0)),
                  pl.BlockSpec((C, LANES), lambda c: (c, 0)),
                  pl.BlockSpec((C, n_heads * dv), lambda c: (c, 0)),
                  pl.BlockSpec(conv_w.shape, lambda c: (0, 0)),
                  pl.BlockSpec((1, LANES), lambda c: (0, 0)),
                  pl.BlockSpec((1, LANES), lambda c: (0, 0)),
                  pl.BlockSpec((1, dv), lambda c: (0, 0))],
        out_specs=pl.BlockSpec((C, n_heads * dv), lambda c: (c, 0)),
        out_shape=jax.ShapeDtypeStruct((L, n_heads * dv), BF16),
        scratch_shapes=[pltpu.VMEM((C + CONV_HALO, cw), F32), pltpu.VMEM((n_heads, dk, dv), F32)],
        compiler_params=_cparams(("arbitrary",)),
        name="gated_deltanet",
    )(qkv, ab, zg, conv_w.astype(F32), pad(a_log), pad(dt_bias), norm_w.astype(F32).reshape(1, dv))


def kernel(x, w_in, ssm_a_re, ssm_a_im, ssm_log_dt, ssm_b_re, ssm_b_im, ssm_c_re, ssm_c_im, ssm_d, w_glu, conv_w, gdn_a_log, gdn_dt_bias, gdn_norm_w, w_gdn_out, w_out, mix_pre_w, mix_post_w, ffn_pre_w, ffn_post_w, w_ff1, w_ff2):
    depth = w_in.shape[0]
    B, L, D = x.shape
    assert B == 1
    n_heads = gdn_a_log.shape[1]
    dv = gdn_norm_w.shape[1]
    gdn_val = w_gdn_out.shape[1]
    assert gdn_val == n_heads * dv
    conv_ch = conv_w.shape[2]
    gdn_key = (conv_ch - gdn_val) // 2
    dk = gdn_key // n_heads
    ssm_w = ssm_d.shape[1]
    d_ff = w_ff1.shape[2]
    o_u = 0
    o_qkv = o_u + ssm_w
    o_a = o_qkv + conv_ch
    o_b = o_a + n_heads
    o_z = o_b + n_heads
    o_gs = o_z + gdn_val
    o_gd = o_gs + D
    assert o_gd + D == w_in.shape[2]
    assert 2 * n_heads <= LANES
    tm = min(MM_TM, L)
    mm = functools.partial(_matmul, tm=tm, tn=MM_TN)
    w_in_t = jnp.swapaxes(w_in, 1, 2)

    xs = x.reshape(L, D)
    h = _prenorm(xs, mix_pre_w[0])
    for i in range(depth):
        u = mm(h, [(w_in_t, i, o_u)], n=ssm_w, out_dtype=F32, tk=D, nt=True, name="in_proj_u")
        qkv = mm(h, [(w_in_t, i, o_qkv)], n=conv_ch, out_dtype=BF16, tk=D, nt=True, name="in_proj_qkv")
        ab = _matmul(h, [(w_in_t, i, o_a)], n=LANES, out_dtype=F32, tm=tm, tn=LANES, tk=D, nt=True,
                     name="in_proj_ab")
        zg = mm(h, [(w_in_t, i, o_z)], n=gdn_val + 2 * D, out_dtype=BF16, tk=D, nt=True,
                name="in_proj_zg")

        tables = _ssm_tables(ssm_a_re[i], ssm_a_im[i], ssm_log_dt[i], ssm_b_re[i], ssm_b_im[i],
                             ssm_c_re[i], ssm_c_im[i], ssm_d[i])
        y = _ssm(u, tables)
        y_ssm = mm(y, [(w_glu, i, 0), (w_glu, i, D)], n=D, out_dtype=BF16, tk=ssm_w,
                   name="glu", epilogue=_glu_epilogue)

        o = _gdn(qkv, ab, zg, conv_w[i], gdn_a_log[i], gdn_dt_bias[i], gdn_norm_w[i],
                 n_heads=n_heads, dk=dk, dv=dv)
        merged = mm(o, [(w_gdn_out, i, 0)], n=D, out_dtype=BF16, tk=gdn_val,
                    name="gdn_out_merge", epilogue=_merge_epilogue,
                    epi=[(zg, gdn_val // MM_TN), (zg, (gdn_val + D) // MM_TN), (y_ssm, 0)])
        hm = mm(merged, [(w_out, i, 0)], n=D, out_dtype=BF16, tk=D, name="out_proj")
        xs, hn = _resnorm(hm, xs, mix_post_w[i], ffn_pre_w[i])

        f1 = mm(hn, [(w_ff1, i, 0)], n=d_ff, out_dtype=BF16, tk=D, name="ff1", epilogue=_relu2_epilogue)
        f2 = _matmul(f1, [(w_ff2, i, 0)], n=D, out_dtype=BF16, tm=tm, tn=MM_TN_FF2, tk=MM_TK_FF2, name="ff2")
        xs, h = _resnorm(f2, xs, ffn_post_w[i], mix_pre_w[i + 1] if i + 1 < depth else None)
    return xs.reshape(B, L, D)
```

```python
import functools
import math

import jax
import jax.numpy as jnp
import numpy as np
from jax import lax
from jax.experimental import pallas as pl
from jax.experimental.pallas import tpu as pltpu

F32 = jnp.float32
BF16 = jnp.bfloat16
EPS = 1e-6

V7X_VMEM_LIMIT_BYTES = 56 * 1024 * 1024
LANES = 128
SSM_FOLD = 8
GDN_CHUNK = 64
CONV_HALO = 8
MM_TM = 1024
MM_TM_CONV = 512
MM_TN = 1024
MM_TN_FF2 = 512
MM_TK_FF2 = 4096


def _cparams(sem):
    return pltpu.CompilerParams(dimension_semantics=sem, vmem_limit_bytes=V7X_VMEM_LIMIT_BYTES)


def _rms(xf, w):
    return xf * lax.rsqrt(jnp.mean(xf * xf, axis=-1, keepdims=True) + EPS) * w


def _sigmoid(x):
    return 1.0 / (1.0 + jnp.exp(-x))


def _prenorm_kernel(x_ref, w_ref, o_ref):
    o_ref[...] = _rms(x_ref[...], w_ref[...]).astype(o_ref.dtype)


def _prenorm(x, w, *, tr=256):
    L, D = x.shape
    return pl.pallas_call(
        _prenorm_kernel,
        grid=(L // tr,),
        in_specs=[pl.BlockSpec((tr, D), lambda i: (i, 0)), pl.BlockSpec((1, D), lambda i: (0, 0))],
        out_specs=pl.BlockSpec((tr, D), lambda i: (i, 0)),
        out_shape=jax.ShapeDtypeStruct((L, D), BF16),
        compiler_params=_cparams(("parallel",)),
        name="prenorm",
    )(x, w.reshape(1, D))


def _resnorm_kernel(h_ref, x_ref, wpost_ref, wpre_ref, xo_ref, ho_ref):
    xn = x_ref[...] + _rms(h_ref[...].astype(F32), wpost_ref[...])
    xo_ref[...] = xn
    ho_ref[...] = _rms(xn, wpre_ref[...]).astype(ho_ref.dtype)


def _resnorm_last_kernel(h_ref, x_ref, wpost_ref, xo_ref):
    xo_ref[...] = x_ref[...] + _rms(h_ref[...].astype(F32), wpost_ref[...])


def _resnorm(h, x, w_post, w_pre, *, tr=256):
    L, D = x.shape
    row = pl.BlockSpec((tr, D), lambda i: (i, 0))
    vec = pl.BlockSpec((1, D), lambda i: (0, 0))
    if w_pre is None:
        return pl.pallas_call(
            _resnorm_last_kernel, grid=(L // tr,), in_specs=[row, row, vec], out_specs=row,
            out_shape=jax.ShapeDtypeStruct((L, D), F32),
            compiler_params=_cparams(("parallel",)), name="resnorm_last",
        )(h, x, w_post.reshape(1, D)), None
    return pl.pallas_call(
        _resnorm_kernel, grid=(L // tr,), in_specs=[row, row, vec, vec], out_specs=[row, row],
        out_shape=[jax.ShapeDtypeStruct((L, D), F32), jax.ShapeDtypeStruct((L, D), BF16)],
        compiler_params=_cparams(("parallel",)), name="resnorm",
    )(h, x, w_post.reshape(1, D), w_pre.reshape(1, D))


def _mm_kernel(*refs, n_w, n_epi, nk, n_chunk, tm, tn, tk, rc, nt, w_index, epilogue, post):
    a_ref = refs[0]
    w_hbm = refs[1:1 + n_w]
    epi_refs = refs[1 + n_w:1 + n_w + n_epi]
    o_ref = refs[1 + n_w + n_epi]
    scratch = refs[2 + n_w + n_epi:]
    wb_refs = scratch[:n_w]
    stage_refs = scratch[n_w:2 * n_w]
    sem = scratch[2 * n_w]
    post_refs = scratch[2 * n_w + 1 + (n_w if nk > 1 else 0):]
    acc_refs = scratch[2 * n_w + 1:2 * n_w + 1 + (n_w if nk > 1 else 0)]
    j = pl.program_id(0)
    k = pl.program_id(1)
    i = pl.program_id(2)
    b = j * nk + k
    slot = lax.rem(b, 2)
    n_blocks = pl.num_programs(0) * nk

    def chunk_copy(w, blk, c, sslot):
        jj, kk = (blk, 0) if nk == 1 else (blk // nk, lax.rem(blk, nk))
        layer, off = w_index[w]
        if nt:
            src = w_hbm[w].at[layer, pl.ds(off + jj * tn + c * rc, rc), pl.ds(kk * tk, tk)]
        else:
            src = w_hbm[w].at[layer, pl.ds(kk * tk + c * rc, rc), pl.ds(off + jj * tn, tn)]
        return pltpu.make_async_copy(src, stage_refs[w].at[sslot], sem.at[w, sslot])

    def round_chunk(w, c, sslot, dslot):
        wb_refs[w][dslot, pl.ds(pl.multiple_of(c * rc, rc), rc), :] = stage_refs[w][sslot].astype(BF16)

    n_i = pl.num_programs(2)
    t = b * n_i + i
    tslot = lax.rem(t, 2)

    @pl.when(t == 0)
    def _():
        if post is not None:
            post[2](post_refs)
        steps = (tn if nt else tk) // rc
        for w in range(n_w):
            chunk_copy(w, 0, 0, 0).start()
            for c in range(steps):
                if c + 1 < steps:
                    chunk_copy(w, 0, c + 1, (c + 1) % 2).start()
                chunk_copy(w, 0, c, c % 2).wait()
                round_chunk(w, c, c % 2, 0)

        @pl.when(n_blocks > 1)
        def _():
            for w in range(n_w):
                chunk_copy(w, 1, 0, 0).start()

    if post is not None:
        post[0](o_ref, epi_refs, post_refs, j, i)
    a = a_ref[...].astype(BF16)
    if nt:
        parts = [_dot_nt(a, wb[slot]) for wb in wb_refs]
    else:
        parts = [jnp.dot(a, wb[slot], preferred_element_type=F32) for wb in wb_refs]

    @pl.when(jnp.logical_and(b + 1 < n_blocks, i < n_chunk))
    def _():
        for w in range(n_w):
            chunk_copy(w, b + 1, i, tslot).wait()
            round_chunk(w, i, tslot, 1 - slot)

    wrap = i + 1 == n_i
    b_next = jnp.where(wrap, b + 1, b)
    i_next = jnp.where(wrap, 0, i + 1)

    @pl.when(jnp.logical_and(b_next + 1 < n_blocks, i_next < n_chunk))
    def _():
        for w in range(n_w):
            chunk_copy(w, b_next + 1, i_next, 1 - tslot).start()

    def finish(accs):
        if post is not None:
            post[1](accs, post_refs, i)
        else:
            o_ref[...] = epilogue(*accs, *[r[...] for r in epi_refs]).astype(o_ref.dtype)

    if nk == 1:
        finish(parts)
        return
    rows = pl.ds(pl.multiple_of(i * tm, tm), tm)

    @pl.when(k == 0)
    def _():
        for acc, part in zip(acc_refs, parts):
            acc[rows, :] = part

    @pl.when(jnp.logical_and(k > 0, k < nk - 1))
    def _():
        for acc, part in zip(acc_refs, parts):
            acc[rows, :] += part

    @pl.when(k == nk - 1)
    def _():
        finish([acc[rows, :] + part for acc, part in zip(acc_refs, parts)])


def _matmul(a, ws, *, n, out_dtype, tm, tn, tk, name, nt=False, epilogue=None, epi=(), post=None, post_scratch=()):
    M, K = a.shape
    nk = K // tk
    n_i = M // tm
    assert M % tm == 0 and n % tn == 0 and K % tk == 0
    rc = (tn if nt else tk) // n_i
    assert rc * n_i == (tn if nt else tk) and rc % 16 == 0
    if epilogue is None:
        epilogue = lambda acc: acc
    if post is not None:
        assert nk == 1
        a_map = lambda j, k, i: (jnp.minimum(i, n_i - 1), k)
    else:
        a_map = lambda j, k, i: (i, k)
    in_specs = [pl.BlockSpec((tm, tk), a_map)]
    args = [a]
    for w, _, _ in ws:
        in_specs.append(pl.BlockSpec(memory_space=pl.ANY))
        args.append(w)
    for e, *spec in epi:
        if len(spec) == 1:
            in_specs.append(pl.BlockSpec((tm, tn), lambda j, k, i, off=spec[0]: (i, off + j)))
        else:
            in_specs.append(pl.BlockSpec(*spec))
        args.append(e)
    wshape = (2, tn, tk) if nt else (2, tk, tn)
    scratch = [pltpu.VMEM(wshape, BF16) for _ in ws]
    scratch += [pltpu.VMEM((2, rc, wshape[2]), F32) for _ in ws]
    scratch += [pltpu.SemaphoreType.DMA((len(ws), 2))]
    if nk > 1:
        scratch += [pltpu.VMEM((M, tn), F32) for _ in ws]
        out_map = lambda j, k, i: (jnp.where(k == nk - 1, i, 0), j)
    elif post is not None:
        out_map = lambda j, k, i: (jnp.maximum(i - 1, 0), j)
    else:
        out_map = lambda j, k, i: (i, j)
    scratch += list(post_scratch)
    kern = functools.partial(_mm_kernel, n_w=len(ws), n_epi=len(epi), nk=nk, n_chunk=n_i, tm=tm, tn=tn, tk=tk, rc=rc,
                             nt=nt,
                             w_index=tuple((layer, off) for _, layer, off in ws), epilogue=epilogue, post=post)
    return pl.pallas_call(
        kern,
        grid=(n // tn, nk, n_i + (post is not None)),
        in_specs=in_specs,
        out_specs=pl.BlockSpec((tm, tn), out_map),
        out_shape=jax.ShapeDtypeStruct((M, n), out_dtype),
        scratch_shapes=scratch,
        compiler_params=_cparams(("arbitrary", "arbitrary", "arbitrary")),
        name=name,
    )(*args)


def _glu_epilogue(acc_a, acc_b):
    return acc_a * _sigmoid(acc_b)


def _merge_epilogue(acc, gate_s, gate_d, y_ssm):
    return (_sigmoid(gate_s.astype(F32)) * y_ssm.astype(F32)
            + _sigmoid(gate_d.astype(F32)) * acc)


def _relu2_epilogue(acc):
    r = jnp.maximum(acc, 0.0)
    return r * r


def _dot_nt(a, b):
    return lax.dot_general(a, b, (((1,), (1,)), ((), ())), preferred_element_type=F32)


def _dot_tn(a, b):
    return lax.dot_general(a, b, (((0,), (0,)), ((), ())), preferred_element_type=F32)


def _ssm_tables(a_re, a_im, log_dt, b_re, b_im, c_re, c_im, d_skip):
    T = SSM_FOLD
    G, N = a_re.shape
    H = b_re.shape[-1]
    gps = LANES // H
    n_slab = G // gps
    dt = jnp.exp(log_dt.astype(F32))[:, None]
    ar = a_re.astype(F32)
    ai = a_im.astype(F32)
    mag = jnp.exp(ar * dt)
    abar_r = mag * jnp.cos(ai * dt)
    abar_i = mag * jnp.sin(ai * dt)
    den = ar * ar + ai * ai
    fr = ((abar_r - 1.0) * ar + abar_i * ai) / den
    fi = (abar_i * ar - (abar_r - 1.0) * ai) / den
    br = b_re.astype(F32)
    bi = b_im.astype(F32)
    bbar_r = fr[..., None] * br - fi[..., None] * bi
    bbar_i = fr[..., None] * bi + fi[..., None] * br
    p = jnp.arange(T + 1, dtype=F32)[:, None, None]
    pmag = jnp.exp(p * (ar * dt)[None])
    pow_r = pmag * jnp.cos(p * (ai * dt)[None])
    pow_i = pmag * jnp.sin(p * (ai * dt)[None])
    cr = c_re.astype(F32)
    ci = c_im.astype(F32)
    rows = T * gps * H

    def compact(t):
        t = t.reshape(2, T, n_slab, gps, H, N).transpose(2, 0, 1, 3, 4, 5).reshape(n_slab, 2, rows, N)
        return jnp.concatenate([t, t], axis=-1).astype(BF16)

    pw_r = pow_r[T - 1 - jnp.arange(T)]
    pw_i = pow_i[T - 1 - jnp.arange(T)]
    pb_r = pw_r[..., None] * bbar_r[None] - pw_i[..., None] * bbar_i[None]
    pb_i = pw_r[..., None] * bbar_i[None] + pw_i[..., None] * bbar_r[None]
    pc = compact(jnp.stack([pb_r, pb_i], axis=0).transpose(0, 1, 2, 4, 3))

    qw_r = pow_r[1:T + 1]
    qw_i = pow_i[1:T + 1]
    ca_r = cr[None] * qw_r[:, :, None, :] - ci[None] * qw_i[:, :, None, :]
    ca_i = cr[None] * qw_i[:, :, None, :] + ci[None] * qw_r[:, :, None, :]
    qc = compact(jnp.stack([ca_r, -ca_i], axis=0))

    cb_r = (jnp.einsum("ghn,lgn,gnk->lghk", cr, pow_r[:T], bbar_r)
            - jnp.einsum("ghn,lgn,gnk->lghk", cr, pow_i[:T], bbar_i)
            - jnp.einsum("ghn,lgn,gnk->lghk", ci, pow_r[:T], bbar_i)
            - jnp.einsum("ghn,lgn,gnk->lghk", ci, pow_i[:T], bbar_r))
    ii = jnp.arange(T)[:, None]
    jj = jnp.arange(T)[None, :]
    lag = jnp.clip(jj - ii, 0, T - 1)
    kij = jnp.where((jj >= ii)[:, :, None, None, None], cb_r[lag], 0.0)
    kc = kij.reshape(T, T, n_slab, gps, H, H).transpose(2, 0, 3, 5, 1, 4)
    kc = kc.reshape(n_slab, rows, T * H).astype(BF16)
    src = np.arange(T * H)
    dst = np.arange(rows)
    spread = ((src[:, None] // H == dst[None, :] // (gps * H)) & (src[:, None] % H == dst[None, :] % H))
    spread = jnp.asarray(spread, BF16)

    a_pow = jnp.stack([pow_r[T].reshape(n_slab, gps * N), pow_i[T].reshape(n_slab, gps * N)], axis=1)
    d_tab = jnp.tile(d_skip.astype(F32).reshape(n_slab, 1, LANES), (1, 1, T))
    return pc, qc, kc, spread, a_pow, d_tab


def _ssm_kernel(u_ref, p_ref, q_ref, k_ref, spread_ref, apow_ref, d_ref, o_ref, s_ref, xp_ref, *, group):
    T = SSM_FOLD
    nc = u_ref.shape[0] // T
    ns = s_ref.shape[1]
    half = ns // 2
    fw = T * LANES
    n_state = p_ref.shape[3] // 2
    gps = LANES // group
    row_g = (lax.broadcasted_iota(jnp.int32, (fw, LANES), 0) // group) % gps
    lane = lax.broadcasted_iota(jnp.int32, (fw, LANES), 1)

    def state_table(ref):
        tiles = []
        for c in range(2):
            t = ref[0, c]
            for v in range(half // LANES):
                tiles.append(jnp.where(row_g == v * (LANES // n_state) + lane // n_state, t, jnp.zeros_like(t)))
        return jnp.concatenate(tiles, axis=1)

    p_tab = state_table(p_ref)
    qt_tab = state_table(q_ref)
    m_rep = jnp.dot(k_ref[0], spread_ref[...], preferred_element_type=F32)
    same_g = ((lax.broadcasted_iota(jnp.int32, (fw, fw), 0) // group) % gps
              == (lax.broadcasted_iota(jnp.int32, (fw, fw), 1) // group) % gps)
    m_tab = jnp.where(same_g, m_rep, 0.0).astype(BF16)
    uf32 = jnp.concatenate([u_ref[pl.ds(i, nc, stride=T), :] for i in range(T)], axis=1)
    uf = uf32.astype(BF16)
    s_ref[...] = jnp.dot(uf, p_tab, preferred_element_type=F32)
    ar = apow_ref[0, 0:1, :]
    ai = apow_ref[0, 1:2, :]

    def step(c, carry):
        xr, xi = carry
        xp_ref[pl.ds(c, 1), 0:half] = xr
        xp_ref[pl.ds(c, 1), half:2 * half] = xi
        sr = s_ref[pl.ds(c, 1), 0:half]
        si = s_ref[pl.ds(c, 1), half:2 * half]
        return ar * xr - ai * xi + sr, ar * xi + ai * xr + si

    zero = jnp.zeros((1, half), F32)
    lax.fori_loop(0, nc, step, (zero, zero), unroll=8)
    y = (jnp.dot(uf, m_tab, preferred_element_type=F32)
         + _dot_nt(xp_ref[...].astype(BF16), qt_tab)
         + d_ref[0] * uf32)
    g = jax.nn.gelu(y, approximate=True)
    for j in range(T):
        o_ref[pl.ds(j, nc, stride=T), :] = g[:, j * LANES:(j + 1) * LANES]


def _ssm(u, tables):
    pc, qc, kc, spread, a_pow, d_tab = tables
    L, W = u.shape
    n_slab = W // LANES
    nc = L // SSM_FOLD
    fw = SSM_FOLD * LANES
    ns = 2 * a_pow.shape[2]
    group = LANES * pc.shape[3] // ns
    tab = lambda r, c: pl.BlockSpec((1, r, c), lambda s: (s, 0, 0))
    tab2 = pl.BlockSpec((1, 2, fw, pc.shape[3]), lambda s: (s, 0, 0, 0))
    return pl.pallas_call(
        functools.partial(_ssm_kernel, group=group),
        grid=(n_slab,),
        in_specs=[pl.BlockSpec((L, LANES), lambda s: (0, s)),
                  tab2, tab2, tab(fw, kc.shape[2]), pl.BlockSpec(spread.shape, lambda s: (0, 0)),
                  tab(2, ns // 2), tab(1, fw)],
        out_specs=pl.BlockSpec((L, LANES), lambda s: (0, s)),
        out_shape=jax.ShapeDtypeStruct((L, W), F32),
        scratch_shapes=[pltpu.VMEM((nc, ns), F32), pltpu.VMEM((nc, ns), F32)],
        compiler_params=_cparams(("parallel",)),
        name="s5_ssm",
    )(u, pc, qc, kc, spread, a_pow, d_tab)


def _gdn_kernel(qkv_ref, ab_ref, z_ref, alog_ref, dtb_ref, normw_ref, o_ref, s_ref, *, n_heads, dk, dv):
    C = GDN_CHUNK
    c = pl.program_id(0)

    @pl.when(c == 0)
    def _():
        s_ref[...] = jnp.zeros(s_ref.shape, F32)

    row = lax.broadcasted_iota(jnp.int32, (C, 2 * C), 0)
    col = lax.broadcasted_iota(jnp.int32, (C, 2 * C), 1)
    colm = jnp.where(col >= C, col - C, col)
    causal2 = colm <= row
    strict2 = colm < row
    low_half = col < C
    eye2 = jnp.where(colm == row, 1.0, 0.0)
    tri = jnp.where(lax.broadcasted_iota(jnp.int32, (C, C), 1) <= lax.broadcasted_iota(jnp.int32, (C, C), 0),
                    1.0, 0.0)

    ab = ab_ref[...]
    sp_in = ab + dtb_ref[...]
    softplus = jnp.maximum(sp_in, 0.0) + jnp.log(1.0 + jnp.exp(-jnp.abs(sp_in)))
    g = -jnp.exp(alog_ref[...]) * softplus
    beta = _sigmoid(ab)
    gc = jnp.dot(tri, g, preferred_element_type=F32, precision=lax.Precision.HIGHEST)
    gct = jnp.transpose(jnp.concatenate([gc, gc], axis=0))

    heads = range(n_heads)
    zero_cc = jnp.zeros((C, 2 * C), F32)
    dot = functools.partial(jnp.dot, preferred_element_type=F32)
    st = []
    for h in heads:
        qn = qkv_ref[:, h * dk:(h + 1) * dk].astype(F32)
        kn = qkv_ref[:, (n_heads + h) * dk:(n_heads + h + 1) * dk].astype(F32)
        vh = qkv_ref[:, 2 * n_heads * dk + h * dv:2 * n_heads * dk + (h + 1) * dv].astype(F32)
        gcol = gc[:, h:h + 1]
        grow2 = gct[h:h + 1, :]
        bcol = beta[:, n_heads + h:n_heads + h + 1]
        glast = gc[C - 1:C, h:h + 1]
        egc = jnp.exp(gcol)
        kb = kn * bcol
        r = jnp.concatenate([vh * bcol, kb * egc], axis=1)
        st.append(dict(
            decay2=jnp.exp(jnp.where(causal2, gcol - grow2, -jnp.inf)),
            k2=jnp.concatenate([kn, kn], axis=0).astype(BF16),
            kbq=jnp.concatenate([kb, qn], axis=0).astype(BF16),
            r2=jnp.concatenate([r, jnp.zeros_like(r)], axis=0).astype(BF16),
            qg=qn * egc,
            kd=(kn * jnp.exp(glast - gcol)).astype(BF16),
            eg=jnp.exp(glast)))
    top = lambda x: jnp.concatenate([x, zero_cc], axis=0).astype(BF16)
    sc = [_dot_nt(t["kbq"], t["k2"]) for t in st]
    n2 = [jnp.where(strict2, s_[0:C] * t["decay2"], 0.0) for s_, t in zip(sc, st)]
    qk2 = [jnp.where(causal2 & low_half, s_[C:2 * C] * t["decay2"], 0.0).astype(BF16) for s_, t in zip(sc, st)]
    valid = low_half & strict2
    blk = 2
    w = [jnp.where(low_half, eye2, 0.0) - jnp.where(valid & (row // blk == colm // blk), n, 0.0) for n in n2]
    while blk < C:
        off = valid & (row // (2 * blk) == colm // (2 * blk)) & (row // blk > colm // blk)
        y = [dot(jnp.where(off, n, 0.0).astype(BF16), top(x)) for n, x in zip(n2, w)]
        w = [x - dot(x.astype(BF16), top(y_)) for x, y_ in zip(w, y)]
        blk *= 2
    uw = [dot(x.astype(BF16), t["r2"]) for x, t in zip(w, st)]
    s_old = [s_ref[h] for h in heads]
    ws = [dot(jnp.concatenate([x[:, dv:dv + dk], t["qg"]], axis=0).astype(BF16), s_.astype(BF16))
          for x, t, s_ in zip(uw, st, s_old)]
    vn = [x[:, 0:dv] - y[0:C] for x, y in zip(uw, ws)]
    o = [y[C:2 * C] + dot(q_, jnp.concatenate([v_, v_], axis=0).astype(BF16)) for y, q_, v_ in zip(ws, qk2, vn)]
    s_new = [s_ * t["eg"] + _dot_tn(t["kd"], v_.astype(BF16)) for s_, t, v_ in zip(s_old, st, vn)]
    for h in heads:
        s_ref[h] = s_new[h]
        zh = z_ref[:, h * dv:(h + 1) * dv].astype(F32)
        on = _rms(o[h], normw_ref[...]) * (zh * _sigmoid(zh))
        o_ref[:, h * dv:(h + 1) * dv] = on.astype(o_ref.dtype)


def _conv_stash(accs, post_refs, i):
    (buf,) = post_refs
    buf[lax.rem(i, 2), CONV_HALO:, :] = accs[0]


def _conv_silu_norm_finish(o_ref, epi_refs, post_refs, j, i, *, tm, dk, n_q, n_qk, q_scale):
    (cw_ref,) = epi_refs
    (buf,) = post_refs
    kw = cw_ref.shape[0]
    halo = CONV_HALO
    cur = lax.rem(i, 2)
    prev = 1 - cur
    buf[cur, 0:halo, :] = jnp.where(i > 0, buf[prev, tm:tm + halo, :], 0.0)
    y = buf[prev, pl.ds(halo - (kw - 1), tm), :] * cw_ref[0:1, :]
    for t in range(1, kw):
        y = y + buf[prev, pl.ds(halo - (kw - 1) + t, tm), :] * cw_ref[t:t + 1, :]
    act = y * _sigmoid(y)
    scale = jnp.where(j < n_q, q_scale, 1.0)
    for s in range(act.shape[1] // dk):
        x = act[:, s * dk:(s + 1) * dk]
        inv = lax.rsqrt(jnp.sum(x * x, axis=-1, keepdims=True) + EPS) * scale
        inv = jnp.where(j < n_qk, inv, 1.0)
        o_ref[:, s * dk:(s + 1) * dk] = (x * inv).astype(o_ref.dtype)


def _conv_init(post_refs):
    (buf,) = post_refs
    buf[...] = jnp.zeros(buf.shape, F32)


def _gdn(qkv, ab, zg, a_log, dt_bias, norm_w, *, n_heads, dk, dv):
    L, cw = qkv.shape
    C = GDN_CHUNK
    pad = lambda v: jnp.pad(v.astype(F32), (0, LANES - v.shape[0])).reshape(1, LANES)
    kern = functools.partial(_gdn_kernel, n_heads=n_heads, dk=dk, dv=dv)
    return pl.pallas_call(
        kern,
        grid=(L // C,),
        in_specs=[pl.BlockSpec((C, cw), lambda c: (c, 0)),
                  pl.BlockSpec((C, LANES), lambda c: (c, 0)),
                  pl.BlockSpec((C, n_heads * dv), lambda c: (c, 0)),
                  pl.BlockSpec((1, LANES), lambda c: (0, 0)),
                  pl.BlockSpec((1, LANES), lambda c: (0, 0)),
                  pl.BlockSpec((1, dv), lambda c: (0, 0))],
        out_specs=pl.BlockSpec((C, n_heads * dv), lambda c: (c, 0)),
        out_shape=jax.ShapeDtypeStruct((L, n_heads * dv), BF16),
        scratch_shapes=[pltpu.VMEM((n_heads, dk, dv), F32)],
        compiler_params=_cparams(("arbitrary",)),
        name="gated_deltanet",
    )(qkv, ab, zg, pad(a_log), pad(dt_bias), norm_w.astype(F32).reshape(1, dv))


def kernel(x, w_in, ssm_a_re, ssm_a_im, ssm_log_dt, ssm_b_re, ssm_b_im, ssm_c_re, ssm_c_im, ssm_d, w_glu, conv_w, gdn_a_log, gdn_dt_bias, gdn_norm_w, w_gdn_out, w_out, mix_pre_w, mix_post_w, ffn_pre_w, ffn_post_w, w_ff1, w_ff2):
    depth = w_in.shape[0]
    B, L, D = x.shape
    assert B == 1
    n_heads = gdn_a_log.shape[1]
    dv = gdn_norm_w.shape[1]
    gdn_val = w_gdn_out.shape[1]
    assert gdn_val == n_heads * dv
    conv_ch = conv_w.shape[2]
    gdn_key = (conv_ch - gdn_val) // 2
    dk = gdn_key // n_heads
    ssm_w = ssm_d.shape[1]
    d_ff = w_ff1.shape[2]
    o_u = 0
    o_qkv = o_u + ssm_w
    o_a = o_qkv + conv_ch
    o_b = o_a + n_heads
    o_z = o_b + n_heads
    o_gs = o_z + gdn_val
    o_gd = o_gs + D
    assert o_gd + D == w_in.shape[2]
    assert 2 * n_heads <= LANES
    tm = min(MM_TM, L)
    mm = functools.partial(_matmul, tm=tm, tn=MM_TN)
    w_in_t = jnp.swapaxes(w_in, 1, 2)

    xs = x.reshape(L, D)
    h = _prenorm(xs, mix_pre_w[0])
    for i in range(depth):
        u = mm(h, [(w_in_t, i, o_u)], n=ssm_w, out_dtype=F32, tk=D, nt=True, name="in_proj_u")
        tm_c = min(MM_TM_CONV, L)
        finish_prev = functools.partial(_conv_silu_norm_finish, tm=tm_c, dk=dk, n_q=gdn_key // MM_TN,
                                        n_qk=2 * gdn_key // MM_TN, q_scale=dk ** -0.5)
        qkv = _matmul(h, [(w_in_t, i, o_qkv)], n=conv_ch, out_dtype=BF16, tm=tm_c, tn=MM_TN, tk=D, nt=True,
                      name="in_proj_qkv",
                      epi=[(conv_w[i].astype(F32), (conv_w.shape[1], MM_TN), lambda j, k, r: (0, j))],
                      post=(finish_prev, _conv_stash, _conv_init),
                      post_scratch=[pltpu.VMEM((2, CONV_HALO + tm_c, MM_TN), F32)])
        ab = _matmul(h, [(w_in_t, i, o_a)], n=LANES, out_dtype=F32, tm=tm, tn=LANES, tk=D, nt=True,
                     name="in_proj_ab")
        zg = mm(h, [(w_in_t, i, o_z)], n=gdn_val + 2 * D, out_dtype=BF16, tk=D, nt=True,
                name="in_proj_zg")

        tables = _ssm_tables(ssm_a_re[i], ssm_a_im[i], ssm_log_dt[i], ssm_b_re[i], ssm_b_im[i],
                             ssm_c_re[i], ssm_c_im[i], ssm_d[i])
        y = _ssm(u, tables)
        y_ssm = mm(y, [(w_glu, i, 0), (w_glu, i, D)], n=D, out_dtype=BF16, tk=ssm_w,
                   name="glu", epilogue=_glu_epilogue)

        o = _gdn(qkv, ab, zg, gdn_a_log[i], gdn_dt_bias[i], gdn_norm_w[i], n_heads=n_heads, dk=dk, dv=dv)
        merged = mm(o, [(w_gdn_out, i, 0)], n=D, out_dtype=BF16, tk=gdn_val,
                    name="gdn_out_merge", epilogue=_merge_epilogue,
                    epi=[(zg, gdn_val // MM_TN), (zg, (gdn_val + D) // MM_TN), (y_ssm, 0)])
        hm = mm(merged, [(w_out, i, 0)], n=D, out_dtype=BF16, tk=D, name="out_proj")
        xs, hn = _resnorm(hm, xs, mix_post_w[i], ffn_pre_w[i])

        f1 = mm(hn, [(w_ff1, i, 0)], n=d_ff, out_dtype=BF16, tk=D, name="ff1", epilogue=_relu2_epilogue)
        f2 = _matmul(f1, [(w_ff2, i, 0)], n=D, out_dtype=BF16, tm=tm, tn=MM_TN_FF2, tk=MM_TK_FF2, name="ff2")
        xs, h = _resnorm(f2, xs, ffn_post_w[i], mix_pre_w[i + 1] if i + 1 < depth else None)
    return xs.reshape(B, L, D)
```

```python
import functools
import math

import jax
import jax.numpy as jnp
import numpy as np
from jax import lax
from jax.experimental import pallas as pl
from jax.experimental.pallas import tpu as pltpu

F32 = jnp.float32
BF16 = jnp.bfloat16
EPS = 1e-6

V7X_VMEM_LIMIT_BYTES = 56 * 1024 * 1024
LANES = 128
SSM_FOLD = 8
GDN_CHUNK = 64
CONV_HALO = 8
MM_TM = 1024
MM_TM_CONV = 512
MM_TN = 1024
MM_TN_FF2 = 512
MM_TK_FF2 = 4096


def _cparams(sem):
    return pltpu.CompilerParams(dimension_semantics=sem, vmem_limit_bytes=V7X_VMEM_LIMIT_BYTES)


def _rms(xf, w):
    return xf * lax.rsqrt(jnp.mean(xf * xf, axis=-1, keepdims=True) + EPS) * w


def _sigmoid(x):
    return 1.0 / (1.0 + jnp.exp(-x))


def _prenorm_kernel(x_ref, w_ref, o_ref):
    o_ref[...] = _rms(x_ref[...], w_ref[...]).astype(o_ref.dtype)


def _prenorm(x, w, *, tr=256):
    L, D = x.shape
    return pl.pallas_call(
        _prenorm_kernel,
        grid=(L // tr,),
        in_specs=[pl.BlockSpec((tr, D), lambda i: (i, 0)), pl.BlockSpec((1, D), lambda i: (0, 0))],
        out_specs=pl.BlockSpec((tr, D), lambda i: (i, 0)),
        out_shape=jax.ShapeDtypeStruct((L, D), BF16),
        compiler_params=_cparams(("parallel",)),
        name="prenorm",
    )(x, w.reshape(1, D))


def _resnorm_kernel(h_ref, x_ref, wpost_ref, wpre_ref, xo_ref, ho_ref):
    xn = x_ref[...] + _rms(h_ref[...].astype(F32), wpost_ref[...])
    xo_ref[...] = xn
    ho_ref[...] = _rms(xn, wpre_ref[...]).astype(ho_ref.dtype)


def _resnorm_last_kernel(h_ref, x_ref, wpost_ref, xo_ref):
    xo_ref[...] = x_ref[...] + _rms(h_ref[...].astype(F32), wpost_ref[...])


def _resnorm(h, x, w_post, w_pre, *, tr=256):
    L, D = x.shape
    row = pl.BlockSpec((tr, D), lambda i: (i, 0))
    vec = pl.BlockSpec((1, D), lambda i: (0, 0))
    if w_pre is None:
        return pl.pallas_call(
            _resnorm_last_kernel, grid=(L // tr,), in_specs=[row, row, vec], out_specs=row,
            out_shape=jax.ShapeDtypeStruct((L, D), F32),
            compiler_params=_cparams(("parallel",)), name="resnorm_last",
        )(h, x, w_post.reshape(1, D)), None
    return pl.pallas_call(
        _resnorm_kernel, grid=(L // tr,), in_specs=[row, row, vec, vec], out_specs=[row, row],
        out_shape=[jax.ShapeDtypeStruct((L, D), F32), jax.ShapeDtypeStruct((L, D), BF16)],
        compiler_params=_cparams(("parallel",)), name="resnorm",
    )(h, x, w_post.reshape(1, D), w_pre.reshape(1, D))


def _mm_kernel(*refs, n_w, n_epi, nk, n_chunk, tm, tn, tk, rc, nt, w_index, epilogue, post):
    a_ref = refs[0]
    w_hbm = refs[1:1 + n_w]
    epi_refs = refs[1 + n_w:1 + n_w + n_epi]
    o_ref = refs[1 + n_w + n_epi]
    scratch = refs[2 + n_w + n_epi:]
    wb_refs = scratch[:n_w]
    stage_refs = scratch[n_w:2 * n_w]
    sem = scratch[2 * n_w]
    post_refs = scratch[2 * n_w + 1 + (n_w if nk > 1 else 0):]
    acc_refs = scratch[2 * n_w + 1:2 * n_w + 1 + (n_w if nk > 1 else 0)]
    j = pl.program_id(0)
    k = pl.program_id(1)
    i = pl.program_id(2)
    b = j * nk + k
    slot = lax.rem(b, 2)
    n_blocks = pl.num_programs(0) * nk

    def chunk_copy(w, blk, c, sslot):
        jj, kk = (blk, 0) if nk == 1 else (blk // nk, lax.rem(blk, nk))
        layer, off = w_index[w]
        if nt:
            src = w_hbm[w].at[layer, pl.ds(off + jj * tn + c * rc, rc), pl.ds(kk * tk, tk)]
        else:
            src = w_hbm[w].at[layer, pl.ds(kk * tk + c * rc, rc), pl.ds(off + jj * tn, tn)]
        return pltpu.make_async_copy(src, stage_refs[w].at[sslot], sem.at[w, sslot])

    def round_chunk(w, c, sslot, dslot):
        wb_refs[w][dslot, pl.ds(pl.multiple_of(c * rc, rc), rc), :] = stage_refs[w][sslot].astype(BF16)

    n_i = pl.num_programs(2)
    t = b * n_i + i
    tslot = lax.rem(t, 2)

    @pl.when(t == 0)
    def _():
        if post is not None:
            post[2](post_refs)
        steps = (tn if nt else tk) // rc
        for w in range(n_w):
            chunk_copy(w, 0, 0, 0).start()
            for c in range(steps):
                if c + 1 < steps:
                    chunk_copy(w, 0, c + 1, (c + 1) % 2).start()
                chunk_copy(w, 0, c, c % 2).wait()
                round_chunk(w, c, c % 2, 0)

        @pl.when(n_blocks > 1)
        def _():
            for w in range(n_w):
                chunk_copy(w, 1, 0, 0).start()

    if post is not None:
        post[0](o_ref, epi_refs, post_refs, j, i)
    a = a_ref[...].astype(BF16)
    if nt:
        parts = [_dot_nt(a, wb[slot]) for wb in wb_refs]
    else:
        parts = [jnp.dot(a, wb[slot], preferred_element_type=F32) for wb in wb_refs]

    @pl.when(jnp.logical_and(b + 1 < n_blocks, i < n_chunk))
    def _():
        for w in range(n_w):
            chunk_copy(w, b + 1, i, tslot).wait()
            round_chunk(w, i, tslot, 1 - slot)

    wrap = i + 1 == n_i
    b_next = jnp.where(wrap, b + 1, b)
    i_next = jnp.where(wrap, 0, i + 1)

    @pl.when(jnp.logical_and(b_next + 1 < n_blocks, i_next < n_chunk))
    def _():
        for w in range(n_w):
            chunk_copy(w, b_next + 1, i_next, 1 - tslot).start()

    def finish(accs):
        if post is not None:
            post[1](accs, post_refs, i)
        else:
            o_ref[...] = epilogue(*accs, *[r[...] for r in epi_refs]).astype(o_ref.dtype)

    if nk == 1:
        finish(parts)
        return
    rows = pl.ds(pl.multiple_of(i * tm, tm), tm)

    @pl.when(k == 0)
    def _():
        for acc, part in zip(acc_refs, parts):
            acc[rows, :] = part

    @pl.when(jnp.logical_and(k > 0, k < nk - 1))
    def _():
        for acc, part in zip(acc_refs, parts):
            acc[rows, :] += part

    @pl.when(k == nk - 1)
    def _():
        finish([acc[rows, :] + part for acc, part in zip(acc_refs, parts)])


def _matmul(a, ws, *, n, out_dtype, tm, tn, tk, name, nt=False, epilogue=None, epi=(), post=None, post_scratch=()):
    M, K = a.shape
    nk = K // tk
    n_i = M // tm
    assert M % tm == 0 and n % tn == 0 and K % tk == 0
    rc = (tn if nt else tk) // n_i
    assert rc * n_i == (tn if nt else tk) and rc % 16 == 0
    if epilogue is None:
        epilogue = lambda acc: acc
    if post is not None:
        assert nk == 1
        a_map = lambda j, k, i: (jnp.minimum(i, n_i - 1), k)
    else:
        a_map = lambda j, k, i: (i, k)
    in_specs = [pl.BlockSpec((tm, tk), a_map)]
    args = [a]
    for w, _, _ in ws:
        in_specs.append(pl.BlockSpec(memory_space=pl.ANY))
        args.append(w)
    for e, *spec in epi:
        if len(spec) == 1:
            in_specs.append(pl.BlockSpec((tm, tn), lambda j, k, i, off=spec[0]: (i, off + j)))
        else:
            in_specs.append(pl.BlockSpec(*spec))
        args.append(e)
    wshape = (2, tn, tk) if nt else (2, tk, tn)
    scratch = [pltpu.VMEM(wshape, BF16) for _ in ws]
    scratch += [pltpu.VMEM((2, rc, wshape[2]), F32) for _ in ws]
    scratch += [pltpu.SemaphoreType.DMA((len(ws), 2))]
    if nk > 1:
        scratch += [pltpu.VMEM((M, tn), F32) for _ in ws]
        out_map = lambda j, k, i: (jnp.where(k == nk - 1, i, 0), j)
    elif post is not None:
        out_map = lambda j, k, i: (jnp.maximum(i - 1, 0), j)
    else:
        out_map = lambda j, k, i: (i, j)
    scratch += list(post_scratch)
    kern = functools.partial(_mm_kernel, n_w=len(ws), n_epi=len(epi), nk=nk, n_chunk=n_i, tm=tm, tn=tn, tk=tk, rc=rc,
                             nt=nt,
                             w_index=tuple((layer, off) for _, layer, off in ws), epilogue=epilogue, post=post)
    return pl.pallas_call(
        kern,
        grid=(n // tn, nk, n_i + (post is not None)),
        in_specs=in_specs,
        out_specs=pl.BlockSpec((tm, tn), out_map),
        out_shape=jax.ShapeDtypeStruct((M, n), out_dtype),
        scratch_shapes=scratch,
        compiler_params=_cparams(("arbitrary", "arbitrary", "arbitrary")),
        name=name,
    )(*args)


def _glu_epilogue(acc_a, acc_b):
    return acc_a * _sigmoid(acc_b)


def _merge_epilogue(acc, gate_s, gate_d, y_ssm):
    return (_sigmoid(gate_s.astype(F32)) * y_ssm.astype(F32)
            + _sigmoid(gate_d.astype(F32)) * acc)


def _relu2_epilogue(acc):
    r = jnp.maximum(acc, 0.0)
    return r * r


def _dot_nt(a, b):
    return lax.dot_general(a, b, (((1,), (1,)), ((), ())), preferred_element_type=F32)


def _dot_tn(a, b):
    return lax.dot_general(a, b, (((0,), (0,)), ((), ())), preferred_element_type=F32)


def _ssm_tables(a_re, a_im, log_dt, b_re, b_im, c_re, c_im, d_skip):
    T = SSM_FOLD
    depth, G, N = a_re.shape
    H = b_re.shape[-1]
    gps = LANES // H
    n_slab = G // gps
    rows = T * gps * H
    dt = jnp.exp(log_dt.astype(F32))[..., None]
    ar = a_re.astype(F32)
    ai = a_im.astype(F32)
    mag = jnp.exp(ar * dt)
    abar_r = mag * jnp.cos(ai * dt)
    abar_i = mag * jnp.sin(ai * dt)
    den = ar * ar + ai * ai
    fr = (((abar_r - 1.0) * ar + abar_i * ai) / den)[:, :, None, :]
    fi = ((abar_i * ar - (abar_r - 1.0) * ai) / den)[:, :, None, :]
    br = jnp.swapaxes(b_re.astype(F32), -1, -2)
    bi = jnp.swapaxes(b_im.astype(F32), -1, -2)
    bbar_r = fr * br - fi * bi
    bbar_i = fr * bi + fi * br
    p = jnp.arange(T + 1, dtype=F32)[None, :, None, None]
    pmag = jnp.exp(p * (ar * dt)[:, None])
    pow_r = (pmag * jnp.cos(p * (ai * dt)[:, None]))[:, :, :, None, :]
    pow_i = (pmag * jnp.sin(p * (ai * dt)[:, None]))[:, :, :, None, :]
    cr = c_re.astype(F32)[:, None]
    ci = c_im.astype(F32)[:, None]

    def compact(tr, ti):
        t = jnp.stack([tr, ti], axis=1).reshape(depth, 2, T, n_slab, gps * H * N)
        t = t.transpose(0, 3, 1, 2, 4).reshape(depth, n_slab, 2, rows, N)
        return jnp.concatenate([t, t], axis=-1).astype(BF16)

    pw_r = pow_r[:, T - 1::-1]
    pw_i = pow_i[:, T - 1::-1]
    pc = compact(pw_r * bbar_r[:, None] - pw_i * bbar_i[:, None], pw_r * bbar_i[:, None] + pw_i * bbar_r[:, None])

    ca_r = cr * pow_r - ci * pow_i
    ca_i = cr * pow_i + ci * pow_r
    qc = compact(ca_r[:, 1:], -ca_i[:, 1:])

    hi = lax.Precision.HIGHEST
    kb = (jnp.einsum("dlghn,dgkn->dgklh", ca_r[:, :T], bbar_r, precision=hi)
          - jnp.einsum("dlghn,dgkn->dgklh", ca_i[:, :T], bbar_i, precision=hi))
    kb = kb.reshape(depth, n_slab, gps * H, T * H).astype(BF16)
    src = np.arange(T * H)
    dst = np.arange(rows)
    spread = ((src[:, None] // H == dst[None, :] // (gps * H)) & (src[:, None] % H == dst[None, :] % H))
    spread = jnp.asarray(spread, BF16)

    a_pow = jnp.stack([pow_r[:, T].reshape(depth, n_slab, gps * N), pow_i[:, T].reshape(depth, n_slab, gps * N)],
                      axis=2)
    d_tab = jnp.tile(d_skip.astype(F32).reshape(depth, n_slab, 1, LANES), (1, 1, 1, T))
    return pc, qc, kb, spread, a_pow, d_tab


def _ssm_kernel(u_ref, p_ref, q_ref, k_ref, spread_ref, apow_ref, d_ref, o_ref, s_ref, xp_ref, *, group):
    T = SSM_FOLD
    nc = u_ref.shape[0] // T
    ns = s_ref.shape[1]
    half = ns // 2
    fw = T * LANES
    n_state = p_ref.shape[2] // 2
    gps = LANES // group
    row_g = (lax.broadcasted_iota(jnp.int32, (fw, LANES), 0) // group) % gps
    lane = lax.broadcasted_iota(jnp.int32, (fw, LANES), 1)

    def state_table(ref):
        tiles = []
        for c in range(2):
            t = ref[c]
            for v in range(half // LANES):
                tiles.append(jnp.where(row_g == v * (LANES // n_state) + lane // n_state, t, jnp.zeros_like(t)))
        return jnp.concatenate(tiles, axis=1)

    p_tab = state_table(p_ref)
    qt_tab = state_table(q_ref)
    kb = k_ref[...].astype(F32)
    klane = lax.broadcasted_iota(jnp.int32, kb.shape, 1)
    kc = jnp.concatenate(
        [kb] + [jnp.where(klane >= group * i, pltpu.roll(kb, group * i, axis=1), 0.0) for i in range(1, T)],
        axis=0).astype(BF16)
    m_rep = jnp.dot(kc, spread_ref[...], preferred_element_type=F32)
    same_g = ((lax.broadcasted_iota(jnp.int32, (fw, fw), 0) // group) % gps
              == (lax.broadcasted_iota(jnp.int32, (fw, fw), 1) // group) % gps)
    m_tab = jnp.where(same_g, m_rep, 0.0).astype(BF16)
    uf32 = jnp.concatenate([u_ref[pl.ds(i, nc, stride=T), :] for i in range(T)], axis=1)
    uf = uf32.astype(BF16)
    s_ref[...] = jnp.dot(uf, p_tab, preferred_element_type=F32)
    ar = apow_ref[0:1, :]
    ai = apow_ref[1:2, :]

    def step(c, carry):
        xr, xi = carry
        xp_ref[pl.ds(c, 1), 0:half] = xr
        xp_ref[pl.ds(c, 1), half:2 * half] = xi
        sr = s_ref[pl.ds(c, 1), 0:half]
        si = s_ref[pl.ds(c, 1), half:2 * half]
        return ar * xr - ai * xi + sr, ar * xi + ai * xr + si

    zero = jnp.zeros((1, half), F32)
    lax.fori_loop(0, nc, step, (zero, zero), unroll=8)
    y = (jnp.dot(uf, m_tab, preferred_element_type=F32)
         + _dot_nt(xp_ref[...].astype(BF16), qt_tab)
         + d_ref[...] * uf32)
    g = jax.nn.gelu(y, approximate=True)
    for j in range(T):
        o_ref[pl.ds(j, nc, stride=T), :] = g[:, j * LANES:(j + 1) * LANES]


def _ssm(u, tables, layer):
    pc, qc, kb, spread, a_pow, d_tab = tables
    L, W = u.shape
    n_slab = W // LANES
    nc = L // SSM_FOLD
    fw = SSM_FOLD * LANES
    ns = 2 * a_pow.shape[3]
    group = LANES * pc.shape[4] // ns
    tab = lambda *shape: pl.BlockSpec((None, None) + shape, lambda s: (layer, s) + (0,) * len(shape))
    return pl.pallas_call(
        functools.partial(_ssm_kernel, group=group),
        grid=(n_slab,),
        in_specs=[pl.BlockSpec((L, LANES), lambda s: (0, s)),
                  tab(2, fw, pc.shape[4]), tab(2, fw, qc.shape[4]), tab(*kb.shape[2:]),
                  pl.BlockSpec(spread.shape, lambda s: (0, 0)),
                  tab(2, ns // 2), tab(1, fw)],
        out_specs=pl.BlockSpec((L, LANES), lambda s: (0, s)),
        out_shape=jax.ShapeDtypeStruct((L, W), F32),
        scratch_shapes=[pltpu.VMEM((nc, ns), F32), pltpu.VMEM((nc, ns), F32)],
        compiler_params=_cparams(("parallel",)),
        name="s5_ssm",
    )(u, pc, qc, kb, spread, a_pow, d_tab)


def _gdn_kernel(qkv_ref, ab_ref, z_ref, alog_ref, dtb_ref, normw_ref, o_ref, s_ref, *, n_heads, dk, dv):
    C = GDN_CHUNK
    c = pl.program_id(0)

    @pl.when(c == 0)
    def _():
        s_ref[...] = jnp.zeros(s_ref.shape, F32)

    row = lax.broadcasted_iota(jnp.int32, (C, 2 * C), 0)
    col = lax.broadcasted_iota(jnp.int32, (C, 2 * C), 1)
    colm = jnp.where(col >= C, col - C, col)
    causal2 = colm <= row
    strict2 = colm < row
    low_half = col < C
    eye2 = jnp.where(colm == row, 1.0, 0.0)
    tri = jnp.where(lax.broadcasted_iota(jnp.int32, (C, C), 1) <= lax.broadcasted_iota(jnp.int32, (C, C), 0),
                    1.0, 0.0)

    ab = ab_ref[...]
    sp_in = ab + dtb_ref[...]
    softplus = jnp.maximum(sp_in, 0.0) + jnp.log(1.0 + jnp.exp(-jnp.abs(sp_in)))
    g = -jnp.exp(alog_ref[...]) * softplus
    beta = _sigmoid(ab)
    gc = jnp.dot(tri, g, preferred_element_type=F32, precision=lax.Precision.HIGHEST)
    gct = jnp.transpose(jnp.concatenate([gc, gc], axis=0))

    heads = range(n_heads)
    zero_cc = jnp.zeros((C, 2 * C), F32)
    dot = functools.partial(jnp.dot, preferred_element_type=F32)
    st = []
    for h in heads:
        qn = qkv_ref[:, h * dk:(h + 1) * dk].astype(F32)
        kn = qkv_ref[:, (n_heads + h) * dk:(n_heads + h + 1) * dk].astype(F32)
        vh = qkv_ref[:, 2 * n_heads * dk + h * dv:2 * n_heads * dk + (h + 1) * dv].astype(F32)
        gcol = gc[:, h:h + 1]
        grow2 = gct[h:h + 1, :]
        bcol = beta[:, n_heads + h:n_heads + h + 1]
        glast = gc[C - 1:C, h:h + 1]
        egc = jnp.exp(gcol)
        kb = kn * bcol
        r = jnp.concatenate([vh * bcol, kb * egc], axis=1)
        st.append(dict(
            decay2=jnp.exp(jnp.where(causal2, gcol - grow2, -jnp.inf)),
            k2=jnp.concatenate([kn, kn], axis=0).astype(BF16),
            kbq=jnp.concatenate([kb, qn], axis=0).astype(BF16),
            r2=jnp.concatenate([r, jnp.zeros_like(r)], axis=0).astype(BF16),
            qg=qn * egc,
            kd=(kn * jnp.exp(glast - gcol)).astype(BF16),
            eg=jnp.exp(glast)))
    top = lambda x: jnp.concatenate([x, zero_cc], axis=0).astype(BF16)
    sc = [_dot_nt(t["kbq"], t["k2"]) for t in st]
    n2 = [jnp.where(strict2, s_[0:C] * t["decay2"], 0.0) for s_, t in zip(sc, st)]
    qk2 = [jnp.where(causal2 & low_half, s_[C:2 * C] * t["decay2"], 0.0).astype(BF16) for s_, t in zip(sc, st)]
    valid = low_half & strict2
    blk = 2
    w = [jnp.where(low_half, eye2, 0.0) - jnp.where(valid & (row // blk == colm // blk), n, 0.0) for n in n2]
    while blk < C:
        off = valid & (row // (2 * blk) == colm // (2 * blk)) & (row // blk > colm // blk)
        y = [dot(jnp.where(off, n, 0.0).astype(BF16), top(x)) for n, x in zip(n2, w)]
        w = [x - dot(x.astype(BF16), top(y_)) for x, y_ in zip(w, y)]
        blk *= 2
    uw = [dot(x.astype(BF16), t["r2"]) for x, t in zip(w, st)]
    s_old = [s_ref[h] for h in heads]
    ws = [dot(jnp.concatenate([x[:, dv:dv + dk], t["qg"]], axis=0).astype(BF16), s_.astype(BF16))
          for x, t, s_ in zip(uw, st, s_old)]
    vn = [x[:, 0:dv] - y[0:C] for x, y in zip(uw, ws)]
    o = [y[C:2 * C] + dot(q_, jnp.concatenate([v_, v_], axis=0).astype(BF16)) for y, q_, v_ in zip(ws, qk2, vn)]
    s_new = [s_ * t["eg"] + _dot_tn(t["kd"], v_.astype(BF16)) for s_, t, v_ in zip(s_old, st, vn)]
    for h in heads:
        s_ref[h] = s_new[h]
        zh = z_ref[:, h * dv:(h + 1) * dv].astype(F32)
        on = _rms(o[h], normw_ref[...]) * (zh * _sigmoid(zh))
        o_ref[:, h * dv:(h + 1) * dv] = on.astype(o_ref.dtype)


def _conv_stash(accs, post_refs, i):
    (buf,) = post_refs
    buf[lax.rem(i, 2), CONV_HALO:, :] = accs[0]


def _conv_silu_norm_finish(o_ref, epi_refs, post_refs, j, i, *, tm, dk, n_q, n_qk, q_scale):
    (cw_ref,) = epi_refs
    (buf,) = post_refs
    kw = cw_ref.shape[0]
    halo = CONV_HALO
    cur = lax.rem(i, 2)
    prev = 1 - cur
    buf[cur, 0:halo, :] = jnp.where(i > 0, buf[prev, tm:tm + halo, :], 0.0)
    ext = buf[prev]
    y = ext[halo:, :] * cw_ref[kw - 1:kw, :]
    for d in range(1, kw):
        y = y + pltpu.roll(ext, d, axis=0)[halo:, :] * cw_ref[kw - 1 - d:kw - d, :]
    act = y * _sigmoid(y)
    scale = jnp.where(j < n_q, q_scale, 1.0)
    for s in range(act.shape[1] // dk):
        x = act[:, s * dk:(s + 1) * dk]
        inv = lax.rsqrt(jnp.sum(x * x, axis=-1, keepdims=True) + EPS) * scale
        inv = jnp.where(j < n_qk, inv, 1.0)
        o_ref[:, s * dk:(s + 1) * dk] = (x * inv).astype(o_ref.dtype)


def _conv_init(post_refs):
    (buf,) = post_refs
    buf[...] = jnp.zeros(buf.shape, F32)


def _gdn(qkv, ab, zg, a_log, dt_bias, norm_w, *, n_heads, dk, dv):
    L, cw = qkv.shape
    C = GDN_CHUNK
    pad = lambda v: jnp.pad(v.astype(F32), (0, LANES - v.shape[0])).reshape(1, LANES)
    kern = functools.partial(_gdn_kernel, n_heads=n_heads, dk=dk, dv=dv)
    return pl.pallas_call(
        kern,
        grid=(L // C,),
        in_specs=[pl.BlockSpec((C, cw), lambda c: (c, 0)),
                  pl.BlockSpec((C, LANES), lambda c: (c, 0)),
                  pl.BlockSpec((C, n_heads * dv), lambda c: (c, 0)),
                  pl.BlockSpec((1, LANES), lambda c: (0, 0)),
                  pl.BlockSpec((1, LANES), lambda c: (0, 0)),
                  pl.BlockSpec((1, dv), lambda c: (0, 0))],
        out_specs=pl.BlockSpec((C, n_heads * dv), lambda c: (c, 0)),
        out_shape=jax.ShapeDtypeStruct((L, n_heads * dv), BF16),
        scratch_shapes=[pltpu.VMEM((n_heads, dk, dv), F32)],
        compiler_params=_cparams(("arbitrary",)),
        name="gated_deltanet",
    )(qkv, ab, zg, pad(a_log), pad(dt_bias), norm_w.astype(F32).reshape(1, dv))


def kernel(x, w_in, ssm_a_re, ssm_a_im, ssm_log_dt, ssm_b_re, ssm_b_im, ssm_c_re, ssm_c_im, ssm_d, w_glu, conv_w, gdn_a_log, gdn_dt_bias, gdn_norm_w, w_gdn_out, w_out, mix_pre_w, mix_post_w, ffn_pre_w, ffn_post_w, w_ff1, w_ff2):
    depth = w_in.shape[0]
    B, L, D = x.shape
    assert B == 1
    n_heads = gdn_a_log.shape[1]
    dv = gdn_norm_w.shape[1]
    gdn_val = w_gdn_out.shape[1]
    assert gdn_val == n_heads * dv
    conv_ch = conv_w.shape[2]
    gdn_key = (conv_ch - gdn_val) // 2
    dk = gdn_key // n_heads
    ssm_w = ssm_d.shape[1]
    d_ff = w_ff1.shape[2]
    o_u = 0
    o_qkv = o_u + ssm_w
    o_a = o_qkv + conv_ch
    o_b = o_a + n_heads
    o_z = o_b + n_heads
    o_gs = o_z + gdn_val
    o_gd = o_gs + D
    assert o_gd + D == w_in.shape[2]
    assert 2 * n_heads <= LANES
    tm = min(MM_TM, L)
    mm = functools.partial(_matmul, tm=tm, tn=MM_TN)
    w_in_t = jnp.swapaxes(w_in, 1, 2)

    xs = x.reshape(L, D)
    h = _prenorm(xs, mix_pre_w[0])
    tables = _ssm_tables(ssm_a_re, ssm_a_im, ssm_log_dt, ssm_b_re, ssm_b_im, ssm_c_re, ssm_c_im, ssm_d)
    for i in range(depth):
        u = mm(h, [(w_in_t, i, o_u)], n=ssm_w, out_dtype=F32, tk=D, nt=True, name="in_proj_u")
        tm_c = min(MM_TM_CONV, L)
        finish_prev = functools.partial(_conv_silu_norm_finish, tm=tm_c, dk=dk, n_q=gdn_key // MM_TN,
                                        n_qk=2 * gdn_key // MM_TN, q_scale=dk ** -0.5)
        qkv = _matmul(h, [(w_in_t, i, o_qkv)], n=conv_ch, out_dtype=BF16, tm=tm_c, tn=MM_TN, tk=D, nt=True,
                      name="in_proj_qkv",
                      epi=[(conv_w[i].astype(F32), (conv_w.shape[1], MM_TN), lambda j, k, r: (0, j))],
                      post=(finish_prev, _conv_stash, _conv_init),
                      post_scratch=[pltpu.VMEM((2, CONV_HALO + tm_c, MM_TN), F32)])
        ab = _matmul(h, [(w_in_t, i, o_a)], n=LANES, out_dtype=F32, tm=tm, tn=LANES, tk=D, nt=True,
                     name="in_proj_ab")
        zg = mm(h, [(w_in_t, i, o_z)], n=gdn_val + 2 * D, out_dtype=BF16, tk=D, nt=True,
                name="in_proj_zg")

        y = _ssm(u, tables, i)
        y_ssm = mm(y, [(w_glu, i, 0), (w_glu, i, D)], n=D, out_dtype=BF16, tk=ssm_w,
                   name="glu", epilogue=_glu_epilogue)

        o = _gdn(qkv, ab, zg, gdn_a_log[i], gdn_dt_bias[i], gdn_norm_w[i], n_heads=n_heads, dk=dk, dv=dv)
        merged = mm(o, [(w_gdn_out, i, 0)], n=D, out_dtype=BF16, tk=gdn_val,
                    name="gdn_out_merge", epilogue=_merge_epilogue,
                    epi=[(zg, gdn_val // MM_TN), (zg, (gdn_val + D) // MM_TN), (y_ssm, 0)])
        hm = mm(merged, [(w_out, i, 0)], n=D, out_dtype=BF16, tk=D, name="out_proj")
        xs, hn = _resnorm(hm, xs, mix_post_w[i], ffn_pre_w[i])

        f1 = mm(hn, [(w_ff1, i, 0)], n=d_ff, out_dtype=BF16, tk=D, name="ff1", epilogue=_relu2_epilogue)
        f2 = _matmul(f1, [(w_ff2, i, 0)], n=D, out_dtype=BF16, tm=tm, tn=MM_TN_FF2, tk=MM_TK_FF2, name="ff2")
        xs, h = _resnorm(f2, xs, ffn_post_w[i], mix_pre_w[i + 1] if i + 1 < depth else None)
    return xs.reshape(B, L, D)
```

```python
import functools
import math

import jax
import jax.numpy as jnp
import numpy as np
from jax import lax
from jax.experimental import pallas as pl
from jax.experimental.pallas import tpu as pltpu

F32 = jnp.float32
BF16 = jnp.bfloat16
EPS = 1e-6

V7X_VMEM_LIMIT_BYTES = 56 * 1024 * 1024
LANES = 128
SSM_FOLD = 8
GDN_CHUNK = 64
CONV_HALO = 8
MM_TM = 1024
MM_TM_CONV = 512
MM_TN = 1024
MM_TN_FF2 = 512
MM_TK_FF2 = 4096


def _cparams(sem):
    return pltpu.CompilerParams(dimension_semantics=sem, vmem_limit_bytes=V7X_VMEM_LIMIT_BYTES)


def _rms(xf, w):
    return xf * lax.rsqrt(jnp.mean(xf * xf, axis=-1, keepdims=True) + EPS) * w


def _sigmoid(x):
    return 1.0 / (1.0 + jnp.exp(-x))


def _prenorm_kernel(x_ref, w_ref, o_ref):
    o_ref[...] = _rms(x_ref[...], w_ref[...]).astype(o_ref.dtype)


def _prenorm(x, w, *, tr=256):
    L, D = x.shape
    return pl.pallas_call(
        _prenorm_kernel,
        grid=(L // tr,),
        in_specs=[pl.BlockSpec((tr, D), lambda i: (i, 0)), pl.BlockSpec((1, D), lambda i: (0, 0))],
        out_specs=pl.BlockSpec((tr, D), lambda i: (i, 0)),
        out_shape=jax.ShapeDtypeStruct((L, D), BF16),
        compiler_params=_cparams(("parallel",)),
        name="prenorm",
    )(x, w.reshape(1, D))


def _resnorm_kernel(h_ref, x_ref, wpost_ref, wpre_ref, xo_ref, ho_ref):
    xn = x_ref[...] + _rms(h_ref[...].astype(F32), wpost_ref[...])
    xo_ref[...] = xn
    ho_ref[...] = _rms(xn, wpre_ref[...]).astype(ho_ref.dtype)


def _resnorm_last_kernel(h_ref, x_ref, wpost_ref, xo_ref):
    xo_ref[...] = x_ref[...] + _rms(h_ref[...].astype(F32), wpost_ref[...])


def _resnorm(h, x, w_post, w_pre, *, tr=256):
    L, D = x.shape
    row = pl.BlockSpec((tr, D), lambda i: (i, 0))
    vec = pl.BlockSpec((1, D), lambda i: (0, 0))
    if w_pre is None:
        return pl.pallas_call(
            _resnorm_last_kernel, grid=(L // tr,), in_specs=[row, row, vec], out_specs=row,
            out_shape=jax.ShapeDtypeStruct((L, D), F32),
            compiler_params=_cparams(("parallel",)), name="resnorm_last",
        )(h, x, w_post.reshape(1, D)), None
    return pl.pallas_call(
        _resnorm_kernel, grid=(L // tr,), in_specs=[row, row, vec, vec], out_specs=[row, row],
        out_shape=[jax.ShapeDtypeStruct((L, D), F32), jax.ShapeDtypeStruct((L, D), BF16)],
        compiler_params=_cparams(("parallel",)), name="resnorm",
    )(h, x, w_post.reshape(1, D), w_pre.reshape(1, D))


def _mm_kernel(*refs, n_w, n_epi, nk, n_chunk, tm, tn, tk, rc, nt, w_index, epilogue, post):
    a_ref = refs[0]
    w_hbm = refs[1:1 + n_w]
    epi_refs = refs[1 + n_w:1 + n_w + n_epi]
    o_ref = refs[1 + n_w + n_epi]
    scratch = refs[2 + n_w + n_epi:]
    wb_even = scratch[:n_w]
    wb_odd = scratch[n_w:2 * n_w]
    stage_refs = scratch[2 * n_w:3 * n_w]
    sem = scratch[3 * n_w]
    n_acc = n_w if nk > 1 else 0
    acc_refs = scratch[3 * n_w + 1:3 * n_w + 1 + n_acc]
    post_refs = scratch[3 * n_w + 1 + n_acc:]
    j = pl.program_id(0)
    k = pl.program_id(1)
    i = pl.program_id(2)
    b = j * nk + k
    n_blocks = pl.num_programs(0) * nk

    def chunk_copy(w, blk, c, sslot):
        jj, kk = (blk, 0) if nk == 1 else (blk // nk, lax.rem(blk, nk))
        layer, off = w_index[w]
        if nt:
            src = w_hbm[w].at[layer, pl.ds(off + jj * tn + c * rc, rc), pl.ds(kk * tk, tk)]
        else:
            src = w_hbm[w].at[layer, pl.ds(kk * tk + c * rc, rc), pl.ds(off + jj * tn, tn)]
        return pltpu.make_async_copy(src, stage_refs[w].at[sslot], sem.at[w, sslot])

    def step_chunk(bb, ii):
        return jnp.minimum(bb + 1, n_blocks - 1), jnp.minimum(ii, n_chunk - 1)

    def round_chunk(w, wb, c, sslot):
        wb[w][pl.ds(pl.multiple_of(c * rc, rc), rc), :] = stage_refs[w][sslot].astype(BF16)

    n_i = pl.num_programs(2)
    t = b * n_i + i
    tslot = lax.rem(t, 2)
    wrap = i + 1 == n_i
    b_next = jnp.where(wrap, b + 1, b)
    i_next = jnp.where(wrap, 0, i + 1)

    @pl.when(t == 0)
    def _():
        if post is not None:
            post[2](post_refs)
        steps = (tn if nt else tk) // rc
        for w in range(n_w):
            chunk_copy(w, 0, 0, 0).start()
            for c in range(steps):
                if c + 1 < steps:
                    chunk_copy(w, 0, c + 1, (c + 1) % 2).start()
                chunk_copy(w, 0, c, c % 2).wait()
                round_chunk(w, wb_even, c, c % 2)
        for w in range(n_w):
            chunk_copy(w, *step_chunk(0, 0), 0).start()

    def finish(accs):
        if post is not None:
            post[1](accs, post_refs, i)
        else:
            o_ref[...] = epilogue(*accs, *[r[...] for r in epi_refs]).astype(o_ref.dtype)

    def body(wb_use, wb_fill):
        if post is not None:
            post[0](o_ref, epi_refs, post_refs, j, i)
        blk, c = step_chunk(b, i)
        for w in range(n_w):
            chunk_copy(w, blk, c, tslot).wait()
            round_chunk(w, wb_fill, c, tslot)
        blk, c = step_chunk(b_next, i_next)
        for w in range(n_w):
            chunk_copy(w, blk, c, 1 - tslot).start()
        a = a_ref[...].astype(BF16)
        if nt:
            parts = [_dot_nt(a, wb[...]) for wb in wb_use]
        else:
            parts = [jnp.dot(a, wb[...], preferred_element_type=F32) for wb in wb_use]
        if nk == 1:
            finish(parts)
            return
        rows = pl.ds(pl.multiple_of(i * tm, tm), tm)

        @pl.when(k == 0)
        def _():
            for acc, part in zip(acc_refs, parts):
                acc[rows, :] = part

        @pl.when(jnp.logical_and(k > 0, k < nk - 1))
        def _():
            for acc, part in zip(acc_refs, parts):
                acc[rows, :] += part

        @pl.when(k == nk - 1)
        def _():
            finish([acc[rows, :] + part for acc, part in zip(acc_refs, parts)])

    even = lax.rem(b, 2) == 0
    pl.when(even)(lambda: body(wb_even, wb_odd))
    pl.when(jnp.logical_not(even))(lambda: body(wb_odd, wb_even))

    @pl.when(jnp.logical_and(b == n_blocks - 1, wrap))
    def _():
        blk, c = step_chunk(b_next, i_next)
        for w in range(n_w):
            chunk_copy(w, blk, c, 1 - tslot).wait()


def _matmul(a, ws, *, n, out_dtype, tm, tn, tk, name, nt=False, epilogue=None, epi=(), post=None, post_scratch=()):
    M, K = a.shape
    nk = K // tk
    n_i = M // tm
    assert M % tm == 0 and n % tn == 0 and K % tk == 0
    rc = (tn if nt else tk) // n_i
    assert rc * n_i == (tn if nt else tk) and rc % 16 == 0
    if epilogue is None:
        epilogue = lambda acc: acc
    if post is not None:
        assert nk == 1
        a_map = lambda j, k, i: (jnp.minimum(i, n_i - 1), k)
    else:
        a_map = lambda j, k, i: (i, k)
    in_specs = [pl.BlockSpec((tm, tk), a_map)]
    args = [a]
    for w, _, _ in ws:
        in_specs.append(pl.BlockSpec(memory_space=pl.ANY))
        args.append(w)
    for e, *spec in epi:
        if len(spec) == 1:
            in_specs.append(pl.BlockSpec((tm, tn), lambda j, k, i, off=spec[0]: (i, off + j)))
        else:
            in_specs.append(pl.BlockSpec(*spec))
        args.append(e)
    wshape = (tn, tk) if nt else (tk, tn)
    scratch = [pltpu.VMEM(wshape, BF16) for _ in ws] * 2
    scratch += [pltpu.VMEM((2, rc, wshape[1]), F32) for _ in ws]
    scratch += [pltpu.SemaphoreType.DMA((len(ws), 2))]
    if nk > 1:
        scratch += [pltpu.VMEM((M, tn), F32) for _ in ws]
        out_map = lambda j, k, i: (jnp.where(k == nk - 1, i, 0), j)
    elif post is not None:
        out_map = lambda j, k, i: (jnp.maximum(i - 1, 0), j)
    else:
        out_map = lambda j, k, i: (i, j)
    scratch += list(post_scratch)
    kern = functools.partial(_mm_kernel, n_w=len(ws), n_epi=len(epi), nk=nk, n_chunk=n_i, tm=tm, tn=tn, tk=tk, rc=rc,
                             nt=nt,
                             w_index=tuple((layer, off) for _, layer, off in ws), epilogue=epilogue, post=post)
    return pl.pallas_call(
        kern,
        grid=(n // tn, nk, n_i + (post is not None)),
        in_specs=in_specs,
        out_specs=pl.BlockSpec((tm, tn), out_map),
        out_shape=jax.ShapeDtypeStruct((M, n), out_dtype),
        scratch_shapes=scratch,
        compiler_params=_cparams(("arbitrary", "arbitrary", "arbitrary")),
        name=name,
    )(*args)


def _glu_epilogue(acc_a, acc_b):
    return acc_a * _sigmoid(acc_b)


def _merge_epilogue(acc, gate_s, gate_d, y_ssm):
    return (_sigmoid(gate_s.astype(F32)) * y_ssm.astype(F32)
            + _sigmoid(gate_d.astype(F32)) * acc)


def _relu2_epilogue(acc):
    r = jnp.maximum(acc, 0.0)
    return r * r


def _dot_nt(a, b):
    return lax.dot_general(a, b, (((1,), (1,)), ((), ())), preferred_element_type=F32)


def _dot_tn(a, b):
    return lax.dot_general(a, b, (((0,), (0,)), ((), ())), preferred_element_type=F32)


def _ssm_tables(a_re, a_im, log_dt, b_re, b_im, c_re, c_im, d_skip):
    T = SSM_FOLD
    depth, G, N = a_re.shape
    H = b_re.shape[-1]
    gps = LANES // H
    n_slab = G // gps
    rows = T * gps * H
    dup = lambda t: jnp.concatenate([t, t], axis=-1)
    dt = jnp.exp(log_dt.astype(F32))[..., None]
    ar = dup(a_re.astype(F32))
    ai = dup(a_im.astype(F32))
    mag = jnp.exp(ar * dt)
    abar_r = mag * jnp.cos(ai * dt)
    abar_i = mag * jnp.sin(ai * dt)
    den = ar * ar + ai * ai
    fr = (((abar_r - 1.0) * ar + abar_i * ai) / den)[:, :, None, :]
    fi = ((abar_i * ar - (abar_r - 1.0) * ai) / den)[:, :, None, :]
    br = dup(jnp.swapaxes(b_re.astype(F32), -1, -2))
    bi = dup(jnp.swapaxes(b_im.astype(F32), -1, -2))
    bbar_r = fr * br - fi * bi
    bbar_i = fr * bi + fi * br
    p = jnp.arange(T + 1, dtype=F32)[None, :, None, None]
    pmag = jnp.exp(p * (ar * dt)[:, None])
    pow_r = (pmag * jnp.cos(p * (ai * dt)[:, None]))[:, :, :, None, :]
    pow_i = (pmag * jnp.sin(p * (ai * dt)[:, None]))[:, :, :, None, :]
    cr = dup(c_re.astype(F32))[:, None]
    ci = dup(c_im.astype(F32))[:, None]

    def compact(tr, ti):
        t = jnp.stack([tr, ti], axis=1).astype(BF16).reshape(depth, 2, T, n_slab, gps * H, 2 * N)
        return t.transpose(0, 3, 1, 2, 4, 5).reshape(depth, n_slab, 2, rows, 2 * N)

    pw_r = pow_r[:, T - 1::-1]
    pw_i = pow_i[:, T - 1::-1]
    pc = compact(pw_r * bbar_r[:, None] - pw_i * bbar_i[:, None], pw_r * bbar_i[:, None] + pw_i * bbar_r[:, None])

    ca_r = cr * pow_r - ci * pow_i
    ca_i = cr * pow_i + ci * pow_r
    qc = compact(ca_r[:, 1:], -ca_i[:, 1:])

    hi = lax.Precision.HIGHEST
    kb = (jnp.einsum("dlghn,dgkn->dgklh", ca_r[:, :T, ..., :N], bbar_r[..., :N], precision=hi)
          - jnp.einsum("dlghn,dgkn->dgklh", ca_i[:, :T, ..., :N], bbar_i[..., :N], precision=hi))
    kb = kb.reshape(depth, n_slab, gps * H, T * H).astype(BF16)
    src = np.arange(T * H)
    dst = np.arange(rows)
    spread = ((src[:, None] // H == dst[None, :] // (gps * H)) & (src[:, None] % H == dst[None, :] % H))
    spread = jnp.asarray(spread, BF16)

    a_pow = jnp.stack([pow_r[:, T, ..., :N].reshape(depth, n_slab, gps * N),
                       pow_i[:, T, ..., :N].reshape(depth, n_slab, gps * N)], axis=2)
    d_tab = jnp.tile(d_skip.astype(F32).reshape(depth, n_slab, 1, LANES), (1, 1, 1, T))
    return pc, qc, kb, spread, a_pow, d_tab


def _ssm_kernel(u_ref, p_ref, q_ref, k_ref, spread_ref, apow_ref, d_ref, o_ref, s_ref, xp_ref, *, group):
    T = SSM_FOLD
    nc = u_ref.shape[0] // T
    ns = s_ref.shape[1]
    half = ns // 2
    fw = T * LANES
    n_state = p_ref.shape[2] // 2
    gps = LANES // group
    row_g = (lax.broadcasted_iota(jnp.int32, (fw, LANES), 0) // group) % gps
    lane = lax.broadcasted_iota(jnp.int32, (fw, LANES), 1)

    def state_table(ref):
        tiles = []
        for c in range(2):
            t = ref[c]
            for v in range(half // LANES):
                tiles.append(jnp.where(row_g == v * (LANES // n_state) + lane // n_state, t, jnp.zeros_like(t)))
        return jnp.concatenate(tiles, axis=1)

    p_tab = state_table(p_ref)
    qt_tab = state_table(q_ref)
    kb = k_ref[...].astype(F32)
    klane = lax.broadcasted_iota(jnp.int32, kb.shape, 1)
    kc = jnp.concatenate(
        [kb] + [jnp.where(klane >= group * i, pltpu.roll(kb, group * i, axis=1), 0.0) for i in range(1, T)],
        axis=0).astype(BF16)
    m_rep = jnp.dot(kc, spread_ref[...], preferred_element_type=F32)
    same_g = ((lax.broadcasted_iota(jnp.int32, (fw, fw), 0) // group) % gps
              == (lax.broadcasted_iota(jnp.int32, (fw, fw), 1) // group) % gps)
    m_tab = jnp.where(same_g, m_rep, 0.0).astype(BF16)
    uf32 = jnp.concatenate([u_ref[pl.ds(i, nc, stride=T), :] for i in range(T)], axis=1)
    uf = uf32.astype(BF16)
    s_ref[...] = jnp.dot(uf, p_tab, preferred_element_type=F32)
    ar = apow_ref[0:1, :]
    ai = apow_ref[1:2, :]

    def step(c, carry):
        xr, xi = carry
        xp_ref[pl.ds(c, 1), 0:half] = xr
        xp_ref[pl.ds(c, 1), half:2 * half] = xi
        sr = s_ref[pl.ds(c, 1), 0:half]
        si = s_ref[pl.ds(c, 1), half:2 * half]
        return ar * xr - ai * xi + sr, ar * xi + ai * xr + si

    zero = jnp.zeros((1, half), F32)
    lax.fori_loop(0, nc, step, (zero, zero), unroll=8)
    y = (jnp.dot(uf, m_tab, preferred_element_type=F32)
         + _dot_nt(xp_ref[...].astype(BF16), qt_tab)
         + d_ref[...] * uf32)
    g = jax.nn.gelu(y, approximate=True)
    for j in range(T):
        o_ref[pl.ds(j, nc, stride=T), :] = g[:, j * LANES:(j + 1) * LANES]


def _ssm(u, tables, layer):
    pc, qc, kb, spread, a_pow, d_tab = tables
    L, W = u.shape
    n_slab = W // LANES
    nc = L // SSM_FOLD
    fw = SSM_FOLD * LANES
    ns = 2 * a_pow.shape[3]
    group = LANES * pc.shape[4] // ns
    tab = lambda *shape: pl.BlockSpec((None, None) + shape, lambda s: (layer, s) + (0,) * len(shape))
    return pl.pallas_call(
        functools.partial(_ssm_kernel, group=group),
        grid=(n_slab,),
        in_specs=[pl.BlockSpec((L, LANES), lambda s: (0, s)),
                  tab(2, fw, pc.shape[4]), tab(2, fw, qc.shape[4]), tab(*kb.shape[2:]),
                  pl.BlockSpec(spread.shape, lambda s: (0, 0)),
                  tab(2, ns // 2), tab(1, fw)],
        out_specs=pl.BlockSpec((L, LANES), lambda s: (0, s)),
        out_shape=jax.ShapeDtypeStruct((L, W), F32),
        scratch_shapes=[pltpu.VMEM((nc, ns), F32), pltpu.VMEM((nc, ns), F32)],
        compiler_params=_cparams(("parallel",)),
        name="s5_ssm",
    )(u, pc, qc, kb, spread, a_pow, d_tab)


def _gdn_kernel(qkv_ref, ab_ref, z_ref, alog_ref, dtb_ref, normw_ref, o_ref, s_ref, *, n_heads, dk, dv):
    C = GDN_CHUNK
    c = pl.program_id(0)

    @pl.when(c == 0)
    def _():
        s_ref[...] = jnp.zeros(s_ref.shape, F32)

    row = lax.broadcasted_iota(jnp.int32, (C, 2 * C), 0)
    col = lax.broadcasted_iota(jnp.int32, (C, 2 * C), 1)
    colm = jnp.where(col >= C, col - C, col)
    causal2 = colm <= row
    strict2 = colm < row
    low_half = col < C
    eye2 = jnp.where(colm == row, 1.0, 0.0)
    tri = jnp.where(lax.broadcasted_iota(jnp.int32, (C, C), 1) <= lax.broadcasted_iota(jnp.int32, (C, C), 0),
                    1.0, 0.0)

    ab = ab_ref[...]
    sp_in = ab + dtb_ref[...]
    softplus = jnp.maximum(sp_in, 0.0) + jnp.log(1.0 + jnp.exp(-jnp.abs(sp_in)))
    g = -jnp.exp(alog_ref[...]) * softplus
    beta = _sigmoid(ab)
    gc = jnp.dot(tri, g, preferred_element_type=F32, precision=lax.Precision.HIGHEST)
    gct = jnp.transpose(jnp.concatenate([gc, gc], axis=0))

    heads = range(n_heads)
    zero_cc = jnp.zeros((C, 2 * C), F32)
    dot = functools.partial(jnp.dot, preferred_element_type=F32)
    st = []
    for h in heads:
        qn = qkv_ref[:, h * dk:(h + 1) * dk].astype(F32)
        kn = qkv_ref[:, (n_heads + h) * dk:(n_heads + h + 1) * dk].astype(F32)
        vh = qkv_ref[:, 2 * n_heads * dk + h * dv:2 * n_heads * dk + (h + 1) * dv].astype(F32)
        gcol = gc[:, h:h + 1]
        grow2 = gct[h:h + 1, :]
        bcol = beta[:, n_heads + h:n_heads + h + 1]
        glast = gc[C - 1:C, h:h + 1]
        egc = jnp.exp(gcol)
        kb = kn * bcol
        r = jnp.concatenate([vh * bcol, kb * egc], axis=1)
        st.append(dict(
            decay2=jnp.exp(jnp.where(causal2, gcol - grow2, -jnp.inf)),
            k2=jnp.concatenate([kn, kn], axis=0).astype(BF16),
            kbq=jnp.concatenate([kb, qn], axis=0).astype(BF16),
            r2=jnp.concatenate([r, jnp.zeros_like(r)], axis=0).astype(BF16),
            qg=qn * egc,
            kd=(kn * jnp.exp(glast - gcol)).astype(BF16),
            eg=jnp.exp(glast)))
    top = lambda x: jnp.concatenate([x, zero_cc], axis=0).astype(BF16)
    sc = [_dot_nt(t["kbq"], t["k2"]) for t in st]
    n2 = [jnp.where(strict2, s_[0:C] * t["decay2"], 0.0) for s_, t in zip(sc, st)]
    qk2 = [jnp.where(causal2 & low_half, s_[C:2 * C] * t["decay2"], 0.0).astype(BF16) for s_, t in zip(sc, st)]
    valid = low_half & strict2
    blk = 2
    w = [jnp.where(low_half, eye2, 0.0) - jnp.where(valid & (row // blk == colm // blk), n, 0.0) for n in n2]
    while blk < C:
        off = valid & (row // (2 * blk) == colm // (2 * blk)) & (row // blk > colm // blk)
        y = [dot(jnp.where(off, n, 0.0).astype(BF16), top(x)) for n, x in zip(n2, w)]
        w = [x - dot(x.astype(BF16), top(y_)) for x, y_ in zip(w, y)]
        blk *= 2
    uw = [dot(x.astype(BF16), t["r2"]) for x, t in zip(w, st)]
    s_old = [s_ref[h] for h in heads]
    ws = [dot(jnp.concatenate([x[:, dv:dv + dk], t["qg"]], axis=0).astype(BF16), s_.astype(BF16))
          for x, t, s_ in zip(uw, st, s_old)]
    vn = [x[:, 0:dv] - y[0:C] for x, y in zip(uw, ws)]
    o = [y[C:2 * C] + dot(q_, jnp.concatenate([v_, v_], axis=0).astype(BF16)) for y, q_, v_ in zip(ws, qk2, vn)]
    s_new = [s_ * t["eg"] + _dot_tn(t["kd"], v_.astype(BF16)) for s_, t, v_ in zip(s_old, st, vn)]
    for h in heads:
        s_ref[h] = s_new[h]
        zh = z_ref[:, h * dv:(h + 1) * dv].astype(F32)
        on = _rms(o[h], normw_ref[...]) * (zh * _sigmoid(zh))
        o_ref[:, h * dv:(h + 1) * dv] = on.astype(o_ref.dtype)


def _conv_stash(accs, post_refs, i):
    (buf,) = post_refs
    buf[lax.rem(i, 2), CONV_HALO:, :] = accs[0]


def _conv_silu_norm_finish(o_ref, epi_refs, post_refs, j, i, *, tm, dk, n_q, n_qk, q_scale):
    (cw_ref,) = epi_refs
    (buf,) = post_refs
    kw = cw_ref.shape[0]
    halo = CONV_HALO
    cur = lax.rem(i, 2)
    prev = 1 - cur
    buf[cur, 0:halo, :] = jnp.where(i > 0, buf[prev, tm:tm + halo, :], 0.0)
    ext = buf[prev]
    y = ext[halo:, :] * cw_ref[kw - 1:kw, :]
    for d in range(1, kw):
        y = y + pltpu.roll(ext, d, axis=0)[halo:, :] * cw_ref[kw - 1 - d:kw - d, :]
    act = y * _sigmoid(y)
    scale = jnp.where(j < n_q, q_scale, 1.0)
    for s in range(act.shape[1] // dk):
        x = act[:, s * dk:(s + 1) * dk]
        inv = lax.rsqrt(jnp.sum(x * x, axis=-1, keepdims=True) + EPS) * scale
        inv = jnp.where(j < n_qk, inv, 1.0)
        o_ref[:, s * dk:(s + 1) * dk] = (x * inv).astype(o_ref.dtype)


def _conv_init(post_refs):
    (buf,) = post_refs
    buf[...] = jnp.zeros(buf.shape, F32)


def _gdn(qkv, ab, zg, a_log, dt_bias, norm_w, *, n_heads, dk, dv):
    L, cw = qkv.shape
    C = GDN_CHUNK
    pad = lambda v: jnp.pad(v.astype(F32), (0, LANES - v.shape[0])).reshape(1, LANES)
    kern = functools.partial(_gdn_kernel, n_heads=n_heads, dk=dk, dv=dv)
    return pl.pallas_call(
        kern,
        grid=(L // C,),
        in_specs=[pl.BlockSpec((C, cw), lambda c: (c, 0)),
                  pl.BlockSpec((C, LANES), lambda c: (c, 0)),
                  pl.BlockSpec((C, n_heads * dv), lambda c: (c, 0)),
                  pl.BlockSpec((1, LANES), lambda c: (0, 0)),
                  pl.BlockSpec((1, LANES), lambda c: (0, 0)),
                  pl.BlockSpec((1, dv), lambda c: (0, 0))],
        out_specs=pl.BlockSpec((C, n_heads * dv), lambda c: (c, 0)),
        out_shape=jax.ShapeDtypeStruct((L, n_heads * dv), BF16),
        scratch_shapes=[pltpu.VMEM((n_heads, dk, dv), F32)],
        compiler_params=_cparams(("arbitrary",)),
        name="gated_deltanet",
    )(qkv, ab, zg, pad(a_log), pad(dt_bias), norm_w.astype(F32).reshape(1, dv))


def kernel(x, w_in, ssm_a_re, ssm_a_im, ssm_log_dt, ssm_b_re, ssm_b_im, ssm_c_re, ssm_c_im, ssm_d, w_glu, conv_w, gdn_a_log, gdn_dt_bias, gdn_norm_w, w_gdn_out, w_out, mix_pre_w, mix_post_w, ffn_pre_w, ffn_post_w, w_ff1, w_ff2):
    depth = w_in.shape[0]
    B, L, D = x.shape
    assert B == 1
    n_heads = gdn_a_log.shape[1]
    dv = gdn_norm_w.shape[1]
    gdn_val = w_gdn_out.shape[1]
    assert gdn_val == n_heads * dv
    conv_ch = conv_w.shape[2]
    gdn_key = (conv_ch - gdn_val) // 2
    dk = gdn_key // n_heads
    ssm_w = ssm_d.shape[1]
    d_ff = w_ff1.shape[2]
    o_u = 0
    o_qkv = o_u + ssm_w
    o_a = o_qkv + conv_ch
    o_b = o_a + n_heads
    o_z = o_b + n_heads
    o_gs = o_z + gdn_val
    o_gd = o_gs + D
    assert o_gd + D == w_in.shape[2]
    assert 2 * n_heads <= LANES
    tm = min(MM_TM, L)
    mm = functools.partial(_matmul, tm=tm, tn=MM_TN)
    w_in_t = jnp.swapaxes(w_in, 1, 2)

    xs = x.reshape(L, D)
    h = _prenorm(xs, mix_pre_w[0])
    tables = _ssm_tables(ssm_a_re, ssm_a_im, ssm_log_dt, ssm_b_re, ssm_b_im, ssm_c_re, ssm_c_im, ssm_d)
    for i in range(depth):
        u = mm(h, [(w_in_t, i, o_u)], n=ssm_w, out_dtype=F32, tk=D, nt=True, name="in_proj_u")
        tm_c = min(MM_TM_CONV, L)
        finish_prev = functools.partial(_conv_silu_norm_finish, tm=tm_c, dk=dk, n_q=gdn_key // MM_TN,
                                        n_qk=2 * gdn_key // MM_TN, q_scale=dk ** -0.5)
        qkv = _matmul(h, [(w_in_t, i, o_qkv)], n=conv_ch, out_dtype=BF16, tm=tm_c, tn=MM_TN, tk=D, nt=True,
                      name="in_proj_qkv",
                      epi=[(conv_w[i].astype(F32), (conv_w.shape[1], MM_TN), lambda j, k, r: (0, j))],
                      post=(finish_prev, _conv_stash, _conv_init),
                      post_scratch=[pltpu.VMEM((2, CONV_HALO + tm_c, MM_TN), F32)])
        ab = _matmul(h, [(w_in_t, i, o_a)], n=LANES, out_dtype=F32, tm=tm, tn=LANES, tk=D, nt=True,
                     name="in_proj_ab")
        zg = mm(h, [(w_in_t, i, o_z)], n=gdn_val + 2 * D, out_dtype=BF16, tk=D, nt=True,
                name="in_proj_zg")

        y = _ssm(u, tables, i)
        y_ssm = mm(y, [(w_glu, i, 0), (w_glu, i, D)], n=D, out_dtype=BF16, tk=ssm_w,
                   name="glu", epilogue=_glu_epilogue)

        o = _gdn(qkv, ab, zg, gdn_a_log[i], gdn_dt_bias[i], gdn_norm_w[i], n_heads=n_heads, dk=dk, dv=dv)
        merged = mm(o, [(w_gdn_out, i, 0)], n=D, out_dtype=BF16, tk=gdn_val,
                    name="gdn_out_merge", epilogue=_merge_epilogue,
                    epi=[(zg, gdn_val // MM_TN), (zg, (gdn_val + D) // MM_TN), (y_ssm, 0)])
        hm = mm(merged, [(w_out, i, 0)], n=D, out_dtype=BF16, tk=D, name="out_proj")
        xs, hn = _resnorm(hm, xs, mix_post_w[i], ffn_pre_w[i])

        f1 = mm(hn, [(w_ff1, i, 0)], n=d_ff, out_dtype=BF16, tk=D, name="ff1", epilogue=_relu2_epilogue)
        f2 = _matmul(f1, [(w_ff2, i, 0)], n=D, out_dtype=BF16, tm=tm, tn=MM_TN_FF2, tk=MM_TK_FF2, name="ff2")
        xs, h = _resnorm(f2, xs, ffn_post_w[i], mix_pre_w[i + 1] if i + 1 < depth else None)
    return xs.reshape(B, L, D)
```

```python
import functools
import math

import jax
import jax.numpy as jnp
import numpy as np
from jax import lax
from jax.experimental import pallas as pl
from jax.experimental.pallas import tpu as pltpu

F32 = jnp.float32
BF16 = jnp.bfloat16
EPS = 1e-6

V7X_VMEM_LIMIT_BYTES = 56 * 1024 * 1024
LANES = 128
SSM_FOLD = 8
GDN_CHUNK = 64
CONV_HALO = 8
MM_TM = 1024
MM_TM_CONV = 512
MM_TN = 1024
MM_TN_FF2 = 512
MM_TK_FF2 = 4096


def _cparams(sem):
    return pltpu.CompilerParams(dimension_semantics=sem, vmem_limit_bytes=V7X_VMEM_LIMIT_BYTES)


def _rms(xf, w):
    return xf * lax.rsqrt(jnp.mean(xf * xf, axis=-1, keepdims=True) + EPS) * w


def _sigmoid(x):
    return 1.0 / (1.0 + jnp.exp(-x))


def _prenorm_kernel(x_ref, w_ref, o_ref):
    o_ref[...] = _rms(x_ref[...], w_ref[...]).astype(o_ref.dtype)


def _prenorm(x, w, *, tr=256):
    L, D = x.shape
    return pl.pallas_call(
        _prenorm_kernel,
        grid=(L // tr,),
        in_specs=[pl.BlockSpec((tr, D), lambda i: (i, 0)), pl.BlockSpec((1, D), lambda i: (0, 0))],
        out_specs=pl.BlockSpec((tr, D), lambda i: (i, 0)),
        out_shape=jax.ShapeDtypeStruct((L, D), BF16),
        compiler_params=_cparams(("parallel",)),
        name="prenorm",
    )(x, w.reshape(1, D))


def _resnorm_kernel(h_ref, x_ref, wpost_ref, wpre_ref, xo_ref, ho_ref):
    xn = x_ref[...] + _rms(h_ref[...].astype(F32), wpost_ref[...])
    xo_ref[...] = xn
    ho_ref[...] = _rms(xn, wpre_ref[...]).astype(ho_ref.dtype)


def _resnorm_last_kernel(h_ref, x_ref, wpost_ref, xo_ref):
    xo_ref[...] = x_ref[...] + _rms(h_ref[...].astype(F32), wpost_ref[...])


def _resnorm(h, x, w_post, w_pre, *, tr=256):
    L, D = x.shape
    row = pl.BlockSpec((tr, D), lambda i: (i, 0))
    vec = pl.BlockSpec((1, D), lambda i: (0, 0))
    if w_pre is None:
        return pl.pallas_call(
            _resnorm_last_kernel, grid=(L // tr,), in_specs=[row, row, vec], out_specs=row,
            out_shape=jax.ShapeDtypeStruct((L, D), F32),
            compiler_params=_cparams(("parallel",)), name="resnorm_last",
        )(h, x, w_post.reshape(1, D)), None
    return pl.pallas_call(
        _resnorm_kernel, grid=(L // tr,), in_specs=[row, row, vec, vec], out_specs=[row, row],
        out_shape=[jax.ShapeDtypeStruct((L, D), F32), jax.ShapeDtypeStruct((L, D), BF16)],
        compiler_params=_cparams(("parallel",)), name="resnorm",
    )(h, x, w_post.reshape(1, D), w_pre.reshape(1, D))


def _mm_kernel(*refs, n_w, n_epi, nk, n_chunk, tm, tn, tk, rc, nt, w_index, epilogue, post):
    a_ref = refs[0]
    w_hbm = refs[1:1 + n_w]
    epi_refs = refs[1 + n_w:1 + n_w + n_epi]
    o_ref = refs[1 + n_w + n_epi]
    scratch = refs[2 + n_w + n_epi:]
    wb_even = scratch[:n_w]
    wb_odd = scratch[n_w:2 * n_w]
    stage_refs = scratch[2 * n_w:3 * n_w]
    sem = scratch[3 * n_w]
    n_acc = n_w if nk > 1 else 0
    acc_refs = scratch[3 * n_w + 1:3 * n_w + 1 + n_acc]
    post_refs = scratch[3 * n_w + 1 + n_acc:]
    j = pl.program_id(0)
    k = pl.program_id(1)
    i = pl.program_id(2)
    b = j * nk + k
    n_blocks = pl.num_programs(0) * nk

    def chunk_copy(w, blk, c, sslot):
        jj, kk = (blk, 0) if nk == 1 else (blk // nk, lax.rem(blk, nk))
        layer, off = w_index[w]
        if nt:
            src = w_hbm[w].at[layer, pl.ds(off + jj * tn + c * rc, rc), pl.ds(kk * tk, tk)]
        else:
            src = w_hbm[w].at[layer, pl.ds(kk * tk + c * rc, rc), pl.ds(off + jj * tn, tn)]
        return pltpu.make_async_copy(src, stage_refs[w].at[sslot], sem.at[w, sslot])

    n_i = n_chunk + (post is not None)
    t = b * n_i + i
    tslot = lax.rem(t, 2)

    def step_chunk(tt):
        return jnp.minimum(tt // n_i + 1, n_blocks - 1), jnp.minimum(lax.rem(tt, n_i), n_chunk - 1)

    def round_chunk(w, wb, c, sslot):
        wb[w][pl.ds(pl.multiple_of(c * rc, rc), rc), :] = stage_refs[w][sslot].astype(BF16)

    @pl.when(t == 0)
    def _():
        if post is not None:
            post[2](post_refs)
        steps = (tn if nt else tk) // rc
        for w in range(n_w):
            chunk_copy(w, 0, 0, 0).start()
            for c in range(steps):
                if c + 1 < steps:
                    chunk_copy(w, 0, c + 1, (c + 1) % 2).start()
                chunk_copy(w, 0, c, c % 2).wait()
                round_chunk(w, wb_even, c, c % 2)
        for w in range(n_w):
            chunk_copy(w, *step_chunk(0), 0).start()
            chunk_copy(w, *step_chunk(1), 1).start()

    def finish(accs):
        if post is not None:
            post[1](accs, post_refs, i)
        else:
            o_ref[...] = epilogue(*accs, *[r[...] for r in epi_refs]).astype(o_ref.dtype)

    def body(wb_use, wb_fill):
        if post is not None:
            post[0](o_ref, epi_refs, post_refs, j, i)
        blk, c = step_chunk(t)
        for w in range(n_w):
            chunk_copy(w, blk, c, tslot).wait()
            round_chunk(w, wb_fill, c, tslot)
        blk, c = step_chunk(t + 2)
        for w in range(n_w):
            chunk_copy(w, blk, c, tslot).start()
        a = a_ref[...].astype(BF16)
        if nt:
            parts = [_dot_nt(a, wb[...]) for wb in wb_use]
        else:
            parts = [jnp.dot(a, wb[...], preferred_element_type=F32) for wb in wb_use]
        if nk == 1:
            finish(parts)
            return
        rows = pl.ds(pl.multiple_of(i * tm, tm), tm)

        @pl.when(k == 0)
        def _():
            for acc, part in zip(acc_refs, parts):
                acc[rows, :] = part

        @pl.when(jnp.logical_and(k > 0, k < nk - 1))
        def _():
            for acc, part in zip(acc_refs, parts):
                acc[rows, :] += part

        @pl.when(k == nk - 1)
        def _():
            finish([acc[rows, :] + part for acc, part in zip(acc_refs, parts)])

    even = lax.rem(b, 2) == 0
    pl.when(even)(lambda: body(wb_even, wb_odd))
    pl.when(jnp.logical_not(even))(lambda: body(wb_odd, wb_even))

    @pl.when(t == n_blocks * n_i - 1)
    def _():
        for ahead in (1, 2):
            blk, c = step_chunk(t + ahead)
            for w in range(n_w):
                chunk_copy(w, blk, c, lax.rem(t + ahead, 2)).wait()


def _matmul(a, ws, *, n, out_dtype, tm, tn, tk, name, nt=False, epilogue=None, epi=(), post=None, post_scratch=()):
    M, K = a.shape
    nk = K // tk
    n_i = M // tm
    assert M % tm == 0 and n % tn == 0 and K % tk == 0
    rc = (tn if nt else tk) // n_i
    assert rc * n_i == (tn if nt else tk) and rc % 16 == 0
    if epilogue is None:
        epilogue = lambda acc: acc
    if post is not None:
        assert nk == 1
        a_map = lambda j, k, i: (jnp.minimum(i, n_i - 1), k)
    else:
        a_map = lambda j, k, i: (i, k)
    in_specs = [pl.BlockSpec((tm, tk), a_map)]
    args = [a]
    for w, _, _ in ws:
        in_specs.append(pl.BlockSpec(memory_space=pl.ANY))
        args.append(w)
    for e, *spec in epi:
        if len(spec) == 1:
            in_specs.append(pl.BlockSpec((tm, tn), lambda j, k, i, off=spec[0]: (i, off + j)))
        else:
            in_specs.append(pl.BlockSpec(*spec))
        args.append(e)
    wshape = (tn, tk) if nt else (tk, tn)
    scratch = [pltpu.VMEM(wshape, BF16) for _ in ws] * 2
    scratch += [pltpu.VMEM((2, rc, wshape[1]), F32) for _ in ws]
    scratch += [pltpu.SemaphoreType.DMA((len(ws), 2))]
    if nk > 1:
        scratch += [pltpu.VMEM((M, tn), F32) for _ in ws]
        out_map = lambda j, k, i: (jnp.where(k == nk - 1, i, 0), j)
    elif post is not None:
        out_map = lambda j, k, i: (jnp.maximum(i - 1, 0), j)
    else:
        out_map = lambda j, k, i: (i, j)
    scratch += list(post_scratch)
    kern = functools.partial(_mm_kernel, n_w=len(ws), n_epi=len(epi), nk=nk, n_chunk=n_i, tm=tm, tn=tn, tk=tk, rc=rc,
                             nt=nt,
                             w_index=tuple((layer, off) for _, layer, off in ws), epilogue=epilogue, post=post)
    return pl.pallas_call(
        kern,
        grid=(n // tn, nk, n_i + (post is not None)),
        in_specs=in_specs,
        out_specs=pl.BlockSpec((tm, tn), out_map),
        out_shape=jax.ShapeDtypeStruct((M, n), out_dtype),
        scratch_shapes=scratch,
        compiler_params=_cparams(("arbitrary", "arbitrary", "arbitrary")),
        name=name,
    )(*args)


def _glu_epilogue(acc_a, acc_b):
    return acc_a * _sigmoid(acc_b)


def _merge_epilogue(acc, gate_s, gate_d, y_ssm):
    return (_sigmoid(gate_s.astype(F32)) * y_ssm.astype(F32)
            + _sigmoid(gate_d.astype(F32)) * acc)


def _relu2_epilogue(acc):
    r = jnp.maximum(acc, 0.0)
    return r * r


def _dot_nt(a, b):
    return lax.dot_general(a, b, (((1,), (1,)), ((), ())), preferred_element_type=F32)


def _dot_tn(a, b):
    return lax.dot_general(a, b, (((0,), (0,)), ((), ())), preferred_element_type=F32)


def _ssm_tables(a_re, a_im, log_dt, b_re, b_im, c_re, c_im, d_skip):
    T = SSM_FOLD
    depth, G, N = a_re.shape
    H = b_re.shape[-1]
    gps = LANES // H
    n_slab = G // gps
    rows = T * gps * H
    dup = lambda t: jnp.concatenate([t, t], axis=-1)
    dt = jnp.exp(log_dt.astype(F32))[..., None]
    ar = dup(a_re.astype(F32))
    ai = dup(a_im.astype(F32))
    mag = jnp.exp(ar * dt)
    abar_r = mag * jnp.cos(ai * dt)
    abar_i = mag * jnp.sin(ai * dt)
    den = ar * ar + ai * ai
    fr = (((abar_r - 1.0) * ar + abar_i * ai) / den)[:, :, None, :]
    fi = ((abar_i * ar - (abar_r - 1.0) * ai) / den)[:, :, None, :]
    br = dup(jnp.swapaxes(b_re.astype(F32), -1, -2))
    bi = dup(jnp.swapaxes(b_im.astype(F32), -1, -2))
    bbar_r = fr * br - fi * bi
    bbar_i = fr * bi + fi * br
    p = jnp.arange(T + 1, dtype=F32)[None, :, None, None]
    pmag = jnp.exp(p * (ar * dt)[:, None])
    pow_r = (pmag * jnp.cos(p * (ai * dt)[:, None]))[:, :, :, None, :]
    pow_i = (pmag * jnp.sin(p * (ai * dt)[:, None]))[:, :, :, None, :]
    pow_r, pow_i, bbar_r, bbar_i = lax.optimization_barrier((pow_r, pow_i, bbar_r, bbar_i))
    cr = dup(c_re.astype(F32))[:, None]
    ci = dup(c_im.astype(F32))[:, None]

    def compact(tr, ti):
        t = jnp.stack([tr, ti], axis=1).astype(BF16).reshape(depth, 2, T, n_slab, gps * H, 2 * N)
        return t.transpose(0, 3, 1, 2, 4, 5).reshape(depth, n_slab, 2, rows, 2 * N)

    pw_r = pow_r[:, T - 1::-1]
    pw_i = pow_i[:, T - 1::-1]
    pc = compact(pw_r * bbar_r[:, None] - pw_i * bbar_i[:, None], pw_r * bbar_i[:, None] + pw_i * bbar_r[:, None])

    ca_r = cr * pow_r - ci * pow_i
    ca_i = cr * pow_i + ci * pow_r
    qc = compact(ca_r[:, 1:], -ca_i[:, 1:])

    hi = lax.Precision.HIGHEST
    kb = (jnp.einsum("dlghn,dgkn->dgklh", ca_r[:, :T, ..., :N], bbar_r[..., :N], precision=hi)
          - jnp.einsum("dlghn,dgkn->dgklh", ca_i[:, :T, ..., :N], bbar_i[..., :N], precision=hi))
    kb = kb.reshape(depth, n_slab, gps * H, T * H).astype(BF16)
    src = np.arange(T * H)
    dst = np.arange(rows)
    spread = ((src[:, None] // H == dst[None, :] // (gps * H)) & (src[:, None] % H == dst[None, :] % H))
    spread = jnp.asarray(spread, BF16)

    a_pow = jnp.stack([pow_r[:, T, ..., :N].reshape(depth, n_slab, gps * N),
                       pow_i[:, T, ..., :N].reshape(depth, n_slab, gps * N)], axis=2)
    d_tab = jnp.tile(d_skip.astype(F32).reshape(depth, n_slab, 1, LANES), (1, 1, 1, T))
    return pc, qc, kb, spread, a_pow, d_tab


def _ssm_kernel(u_ref, p_ref, q_ref, k_ref, spread_ref, apow_ref, d_ref, o_ref, s_ref, xp_ref, *, group):
    T = SSM_FOLD
    nc = u_ref.shape[0] // T
    ns = s_ref.shape[1]
    half = ns // 2
    fw = T * LANES
    n_state = p_ref.shape[2] // 2
    gps = LANES // group
    row_g = (lax.broadcasted_iota(jnp.int32, (fw, LANES), 0) // group) % gps
    lane = lax.broadcasted_iota(jnp.int32, (fw, LANES), 1)

    def state_table(ref):
        tiles = []
        for c in range(2):
            t = ref[c]
            for v in range(half // LANES):
                tiles.append(jnp.where(row_g == v * (LANES // n_state) + lane // n_state, t, jnp.zeros_like(t)))
        return jnp.concatenate(tiles, axis=1)

    p_tab = state_table(p_ref)
    qt_tab = state_table(q_ref)
    kb = k_ref[...].astype(F32)
    klane = lax.broadcasted_iota(jnp.int32, kb.shape, 1)
    kc = jnp.concatenate(
        [kb] + [jnp.where(klane >= group * i, pltpu.roll(kb, group * i, axis=1), 0.0) for i in range(1, T)],
        axis=0).astype(BF16)
    m_rep = jnp.dot(kc, spread_ref[...], preferred_element_type=F32)
    same_g = ((lax.broadcasted_iota(jnp.int32, (fw, fw), 0) // group) % gps
              == (lax.broadcasted_iota(jnp.int32, (fw, fw), 1) // group) % gps)
    m_tab = jnp.where(same_g, m_rep, 0.0).astype(BF16)
    uf32 = jnp.concatenate([u_ref[pl.ds(i, nc, stride=T), :] for i in range(T)], axis=1)
    uf = uf32.astype(BF16)
    s_ref[...] = jnp.dot(uf, p_tab, preferred_element_type=F32)
    ar = apow_ref[0:1, :]
    ai = apow_ref[1:2, :]

    def step(c, carry):
        xr, xi = carry
        xp_ref[pl.ds(c, 1), 0:half] = xr
        xp_ref[pl.ds(c, 1), half:2 * half] = xi
        sr = s_ref[pl.ds(c, 1), 0:half]
        si = s_ref[pl.ds(c, 1), half:2 * half]
        return ar * xr - ai * xi + sr, ar * xi + ai * xr + si

    zero = jnp.zeros((1, half), F32)
    lax.fori_loop(0, nc, step, (zero, zero), unroll=8)
    y = (jnp.dot(uf, m_tab, preferred_element_type=F32)
         + _dot_nt(xp_ref[...].astype(BF16), qt_tab)
         + d_ref[...] * uf32)
    g = jax.nn.gelu(y, approximate=True)
    for j in range(T):
        o_ref[pl.ds(j, nc, stride=T), :] = g[:, j * LANES:(j + 1) * LANES]


def _ssm(u, tables, layer):
    pc, qc, kb, spread, a_pow, d_tab = tables
    L, W = u.shape
    n_slab = W // LANES
    nc = L // SSM_FOLD
    fw = SSM_FOLD * LANES
    ns = 2 * a_pow.shape[3]
    group = LANES * pc.shape[4] // ns
    tab = lambda *shape: pl.BlockSpec((None, None) + shape, lambda s: (layer, s) + (0,) * len(shape))
    return pl.pallas_call(
        functools.partial(_ssm_kernel, group=group),
        grid=(n_slab,),
        in_specs=[pl.BlockSpec((L, LANES), lambda s: (0, s)),
                  tab(2, fw, pc.shape[4]), tab(2, fw, qc.shape[4]), tab(*kb.shape[2:]),
                  pl.BlockSpec(spread.shape, lambda s: (0, 0)),
                  tab(2, ns // 2), tab(1, fw)],
        out_specs=pl.BlockSpec((L, LANES), lambda s: (0, s)),
        out_shape=jax.ShapeDtypeStruct((L, W), F32),
        scratch_shapes=[pltpu.VMEM((nc, ns), F32), pltpu.VMEM((nc, ns), F32)],
        compiler_params=_cparams(("parallel",)),
        name="s5_ssm",
    )(u, pc, qc, kb, spread, a_pow, d_tab)


def _gdn_kernel(qkv_ref, ab_ref, z_ref, alog_ref, dtb_ref, normw_ref, o_ref, s_ref, *, n_heads, dk, dv):
    C = GDN_CHUNK
    c = pl.program_id(0)

    @pl.when(c == 0)
    def _():
        s_ref[...] = jnp.zeros(s_ref.shape, F32)

    row = lax.broadcasted_iota(jnp.int32, (C, 2 * C), 0)
    col = lax.broadcasted_iota(jnp.int32, (C, 2 * C), 1)
    colm = jnp.where(col >= C, col - C, col)
    causal2 = colm <= row
    strict2 = colm < row
    low_half = col < C
    eye2 = jnp.where(colm == row, 1.0, 0.0)
    tri = jnp.where(lax.broadcasted_iota(jnp.int32, (C, C), 1) <= lax.broadcasted_iota(jnp.int32, (C, C), 0),
                    1.0, 0.0)

    ab = ab_ref[...]
    sp_in = ab + dtb_ref[...]
    softplus = jnp.maximum(sp_in, 0.0) + jnp.log(1.0 + jnp.exp(-jnp.abs(sp_in)))
    g = -jnp.exp(alog_ref[...]) * softplus
    beta = _sigmoid(ab)
    gc = jnp.dot(tri, g, preferred_element_type=F32, precision=lax.Precision.HIGHEST)
    gct = jnp.transpose(jnp.concatenate([gc, gc], axis=0))

    heads = range(n_heads)
    zero_cc = jnp.zeros((C, 2 * C), F32)
    dot = functools.partial(jnp.dot, preferred_element_type=F32)
    st = []
    for h in heads:
        qn = qkv_ref[:, h * dk:(h + 1) * dk].astype(F32)
        kn = qkv_ref[:, (n_heads + h) * dk:(n_heads + h + 1) * dk].astype(F32)
        vh = qkv_ref[:, 2 * n_heads * dk + h * dv:2 * n_heads * dk + (h + 1) * dv].astype(F32)
        gcol = gc[:, h:h + 1]
        grow2 = gct[h:h + 1, :]
        bcol = beta[:, n_heads + h:n_heads + h + 1]
        glast = gc[C - 1:C, h:h + 1]
        egc = jnp.exp(gcol)
        kb = kn * bcol
        r = jnp.concatenate([vh * bcol, kb * egc], axis=1)
        st.append(dict(
            decay2=jnp.exp(jnp.where(causal2, gcol - grow2, -jnp.inf)),
            k2=jnp.concatenate([kn, kn], axis=0).astype(BF16),
            kbq=jnp.concatenate([kb, qn], axis=0).astype(BF16),
            r2=jnp.concatenate([r, jnp.zeros_like(r)], axis=0).astype(BF16),
            qg=qn * egc,
            kd=(kn * jnp.exp(glast - gcol)).astype(BF16),
            eg=jnp.exp(glast)))
    top = lambda x: jnp.concatenate([x, zero_cc], axis=0).astype(BF16)
    sc = [_dot_nt(t["kbq"], t["k2"]) for t in st]
    n2 = [jnp.where(strict2, s_[0:C] * t["decay2"], 0.0) for s_, t in zip(sc, st)]
    qk2 = [jnp.where(causal2 & low_half, s_[C:2 * C] * t["decay2"], 0.0).astype(BF16) for s_, t in zip(sc, st)]
    valid = low_half & strict2
    blk = 2
    w = [jnp.where(low_half, eye2, 0.0) - jnp.where(valid & (row // blk == colm // blk), n, 0.0) for n in n2]
    while blk < C:
        off = valid & (row // (2 * blk) == colm // (2 * blk)) & (row // blk > colm // blk)
        y = [dot(jnp.where(off, n, 0.0).astype(BF16), top(x)) for n, x in zip(n2, w)]
        w = [x - dot(x.astype(BF16), top(y_)) for x, y_ in zip(w, y)]
        blk *= 2
    uw = [dot(x.astype(BF16), t["r2"]) for x, t in zip(w, st)]
    s_old = [s_ref[h] for h in heads]
    ws = [dot(jnp.concatenate([x[:, dv:dv + dk], t["qg"]], axis=0).astype(BF16), s_.astype(BF16))
          for x, t, s_ in zip(uw, st, s_old)]
    vn = [x[:, 0:dv] - y[0:C] for x, y in zip(uw, ws)]
    o = [y[C:2 * C] + dot(q_, jnp.concatenate([v_, v_], axis=0).astype(BF16)) for y, q_, v_ in zip(ws, qk2, vn)]
    s_new = [s_ * t["eg"] + _dot_tn(t["kd"], v_.astype(BF16)) for s_, t, v_ in zip(s_old, st, vn)]
    for h in heads:
        s_ref[h] = s_new[h]
        zh = z_ref[:, h * dv:(h + 1) * dv].astype(F32)
        on = _rms(o[h], normw_ref[...]) * (zh * _sigmoid(zh))
        o_ref[:, h * dv:(h + 1) * dv] = on.astype(o_ref.dtype)


def _conv_stash(accs, post_refs, i):
    (buf,) = post_refs
    buf[lax.rem(i, 2), CONV_HALO:, :] = accs[0]


def _conv_silu_norm_finish(o_ref, epi_refs, post_refs, j, i, *, tm, dk, n_q, n_qk, q_scale):
    (cw_ref,) = epi_refs
    (buf,) = post_refs
    kw = cw_ref.shape[0]
    halo = CONV_HALO
    cur = lax.rem(i, 2)
    prev = 1 - cur
    buf[cur, 0:halo, :] = jnp.where(i > 0, buf[prev, tm:tm + halo, :], 0.0)
    ext = buf[prev]
    y = ext[halo:, :] * cw_ref[kw - 1:kw, :]
    for d in range(1, kw):
        y = y + pltpu.roll(ext, d, axis=0)[halo:, :] * cw_ref[kw - 1 - d:kw - d, :]
    act = y * _sigmoid(y)
    scale = jnp.where(j < n_q, q_scale, 1.0)
    for s in range(act.shape[1] // dk):
        x = act[:, s * dk:(s + 1) * dk]
        inv = lax.rsqrt(jnp.sum(x * x, axis=-1, keepdims=True) + EPS) * scale
        inv = jnp.where(j < n_qk, inv, 1.0)
        o_ref[:, s * dk:(s + 1) * dk] = (x * inv).astype(o_ref.dtype)


def _conv_init(post_refs):
    (buf,) = post_refs
    buf[...] = jnp.zeros(buf.shape, F32)


def _gdn(qkv, ab, zg, a_log, dt_bias, norm_w, *, n_heads, dk, dv):
    L, cw = qkv.shape
    C = GDN_CHUNK
    pad = lambda v: jnp.pad(v.astype(F32), (0, LANES - v.shape[0])).reshape(1, LANES)
    kern = functools.partial(_gdn_kernel, n_heads=n_heads, dk=dk, dv=dv)
    return pl.pallas_call(
        kern,
        grid=(L // C,),
        in_specs=[pl.BlockSpec((C, cw), lambda c: (c, 0)),
                  pl.BlockSpec((C, LANES), lambda c: (c, 0)),
                  pl.BlockSpec((C, n_heads * dv), lambda c: (c, 0)),
                  pl.BlockSpec((1, LANES), lambda c: (0, 0)),
                  pl.BlockSpec((1, LANES), lambda c: (0, 0)),
                  pl.BlockSpec((1, dv), lambda c: (0, 0))],
        out_specs=pl.BlockSpec((C, n_heads * dv), lambda c: (c, 0)),
        out_shape=jax.ShapeDtypeStruct((L, n_heads * dv), BF16),
        scratch_shapes=[pltpu.VMEM((n_heads, dk, dv), F32)],
        compiler_params=_cparams(("arbitrary",)),
        name="gated_deltanet",
    )(qkv, ab, zg, pad(a_log), pad(dt_bias), norm_w.astype(F32).reshape(1, dv))


def kernel(x, w_in, ssm_a_re, ssm_a_im, ssm_log_dt, ssm_b_re, ssm_b_im, ssm_c_re, ssm_c_im, ssm_d, w_glu, conv_w, gdn_a_log, gdn_dt_bias, gdn_norm_w, w_gdn_out, w_out, mix_pre_w, mix_post_w, ffn_pre_w, ffn_post_w, w_ff1, w_ff2):
    depth = w_in.shape[0]
    B, L, D = x.shape
    assert B == 1
    n_heads = gdn_a_log.shape[1]
    dv = gdn_norm_w.shape[1]
    gdn_val = w_gdn_out.shape[1]
    assert gdn_val == n_heads * dv
    conv_ch = conv_w.shape[2]
    gdn_key = (conv_ch - gdn_val) // 2
    dk = gdn_key // n_heads
    ssm_w = ssm_d.shape[1]
    d_ff = w_ff1.shape[2]
    o_u = 0
    o_qkv = o_u + ssm_w
    o_a = o_qkv + conv_ch
    o_b = o_a + n_heads
    o_z = o_b + n_heads
    o_gs = o_z + gdn_val
    o_gd = o_gs + D
    assert o_gd + D == w_in.shape[2]
    assert 2 * n_heads <= LANES
    tm = min(MM_TM, L)
    mm = functools.partial(_matmul, tm=tm, tn=MM_TN)
    w_in_t = jnp.swapaxes(w_in, 1, 2)

    xs = x.reshape(L, D)
    h = _prenorm(xs, mix_pre_w[0])
    tables = _ssm_tables(ssm_a_re, ssm_a_im, ssm_log_dt, ssm_b_re, ssm_b_im, ssm_c_re, ssm_c_im, ssm_d)
    for i in range(depth):
        u = mm(h, [(w_in_t, i, o_u)], n=ssm_w, out_dtype=F32, tk=D, nt=True, name="in_proj_u")
        tm_c = min(MM_TM_CONV, L)
        finish_prev = functools.partial(_conv_silu_norm_finish, tm=tm_c, dk=dk, n_q=gdn_key // MM_TN,
                                        n_qk=2 * gdn_key // MM_TN, q_scale=dk ** -0.5)
        qkv = _matmul(h, [(w_in_t, i, o_qkv)], n=conv_ch, out_dtype=BF16, tm=tm_c, tn=MM_TN, tk=D, nt=True,
                      name="in_proj_qkv",
                      epi=[(conv_w[i].astype(F32), (conv_w.shape[1], MM_TN), lambda j, k, r: (0, j))],
                      post=(finish_prev, _conv_stash, _conv_init),
                      post_scratch=[pltpu.VMEM((2, CONV_HALO + tm_c, MM_TN), F32)])
        ab = _matmul(h, [(w_in_t, i, o_a)], n=LANES, out_dtype=F32, tm=tm, tn=LANES, tk=D, nt=True,
                     name="in_proj_ab")
        zg = mm(h, [(w_in_t, i, o_z)], n=gdn_val + 2 * D, out_dtype=BF16, tk=D, nt=True,
                name="in_proj_zg")

        y = _ssm(u, tables, i)
        y_ssm = mm(y, [(w_glu, i, 0), (w_glu, i, D)], n=D, out_dtype=BF16, tk=ssm_w,
                   name="glu", epilogue=_glu_epilogue)

        o = _gdn(qkv, ab, zg, gdn_a_log[i], gdn_dt_bias[i], gdn_norm_w[i], n_heads=n_heads, dk=dk, dv=dv)
        merged = mm(o, [(w_gdn_out, i, 0)], n=D, out_dtype=BF16, tk=gdn_val,
                    name="gdn_out_merge", epilogue=_merge_epilogue,
                    epi=[(zg, gdn_val // MM_TN), (zg, (gdn_val + D) // MM_TN), (y_ssm, 0)])
        hm = mm(merged, [(w_out, i, 0)], n=D, out_dtype=BF16, tk=D, name="out_proj")
        xs, hn = _resnorm(hm, xs, mix_post_w[i], ffn_pre_w[i])

        f1 = mm(hn, [(w_ff1, i, 0)], n=d_ff, out_dtype=BF16, tk=D, name="ff1", epilogue=_relu2_epilogue)
        f2 = _matmul(f1, [(w_ff2, i, 0)], n=D, out_dtype=BF16, tm=tm, tn=MM_TN_FF2, tk=MM_TK_FF2, name="ff2")
        xs, h = _resnorm(f2, xs, ffn_post_w[i], mix_pre_w[i + 1] if i + 1 < depth else None)
    return xs.reshape(B, L, D)
```

```python
import functools
import math

import jax
import jax.numpy as jnp
import numpy as np
from jax import lax
from jax.experimental import pallas as pl
from jax.experimental.pallas import tpu as pltpu

F32 = jnp.float32
BF16 = jnp.bfloat16
EPS = 1e-6

V7X_VMEM_LIMIT_BYTES = 56 * 1024 * 1024
LANES = 128
SSM_FOLD = 8
GDN_CHUNK = 64
CONV_HALO = 8
EPI_ROWS = 16
MM_TM = 1024
MM_TM_CONV = 512
MM_TN = 1024
MM_TN_FF2 = 512
MM_TK_FF2 = 4096


def _cparams(sem):
    return pltpu.CompilerParams(dimension_semantics=sem, vmem_limit_bytes=V7X_VMEM_LIMIT_BYTES)


def _rms(xf, w):
    return xf * lax.rsqrt(jnp.mean(xf * xf, axis=-1, keepdims=True) + EPS) * w


def _sigmoid(x):
    return 1.0 / (1.0 + jnp.exp(-x))


def _prenorm_kernel(x_ref, w_ref, o_ref):
    o_ref[...] = _rms(x_ref[...], w_ref[...]).astype(o_ref.dtype)


def _prenorm(x, w, *, tr=256):
    L, D = x.shape
    return pl.pallas_call(
        _prenorm_kernel,
        grid=(L // tr,),
        in_specs=[pl.BlockSpec((tr, D), lambda i: (i, 0)), pl.BlockSpec((1, D), lambda i: (0, 0))],
        out_specs=pl.BlockSpec((tr, D), lambda i: (i, 0)),
        out_shape=jax.ShapeDtypeStruct((L, D), BF16),
        compiler_params=_cparams(("parallel",)),
        name="prenorm",
    )(x, w.reshape(1, D))


def _resnorm_kernel(h_ref, x_ref, wpost_ref, wpre_ref, xo_ref, ho_ref):
    xn = x_ref[...] + _rms(h_ref[...].astype(F32), wpost_ref[...])
    xo_ref[...] = xn
    ho_ref[...] = _rms(xn, wpre_ref[...]).astype(ho_ref.dtype)


def _resnorm_last_kernel(h_ref, x_ref, wpost_ref, xo_ref):
    xo_ref[...] = x_ref[...] + _rms(h_ref[...].astype(F32), wpost_ref[...])


def _resnorm(h, x, w_post, w_pre, *, tr=256):
    L, D = x.shape
    row = pl.BlockSpec((tr, D), lambda i: (i, 0))
    vec = pl.BlockSpec((1, D), lambda i: (0, 0))
    if w_pre is None:
        return pl.pallas_call(
            _resnorm_last_kernel, grid=(L // tr,), in_specs=[row, row, vec], out_specs=row,
            out_shape=jax.ShapeDtypeStruct((L, D), F32),
            compiler_params=_cparams(("parallel",)), name="resnorm_last",
        )(h, x, w_post.reshape(1, D)), None
    return pl.pallas_call(
        _resnorm_kernel, grid=(L // tr,), in_specs=[row, row, vec, vec], out_specs=[row, row],
        out_shape=[jax.ShapeDtypeStruct((L, D), F32), jax.ShapeDtypeStruct((L, D), BF16)],
        compiler_params=_cparams(("parallel",)), name="resnorm",
    )(h, x, w_post.reshape(1, D), w_pre.reshape(1, D))


def _mm_kernel(*refs, n_w, n_epi, nk, n_chunk, tm, tn, tk, rc, nt, w_index, epilogue, post):
    a_ref = refs[0]
    w_hbm = refs[1:1 + n_w]
    epi_refs = refs[1 + n_w:1 + n_w + n_epi]
    o_ref = refs[1 + n_w + n_epi]
    scratch = refs[2 + n_w + n_epi:]
    wb_even = scratch[:n_w]
    wb_odd = scratch[n_w:2 * n_w]
    stage_refs = scratch[2 * n_w:3 * n_w]
    sem = scratch[3 * n_w]
    n_acc = n_w if nk > 1 else 0
    acc_refs = scratch[3 * n_w + 1:3 * n_w + 1 + n_acc]
    post_refs = scratch[3 * n_w + 1 + n_acc:]
    j = pl.program_id(0)
    k = pl.program_id(1)
    i = pl.program_id(2)
    b = j * nk + k
    n_blocks = pl.num_programs(0) * nk

    def chunk_copy(w, blk, c, sslot):
        jj, kk = (blk, 0) if nk == 1 else (blk // nk, lax.rem(blk, nk))
        layer, off = w_index[w]
        if nt:
            src = w_hbm[w].at[layer, pl.ds(off + jj * tn + c * rc, rc), pl.ds(kk * tk, tk)]
        else:
            src = w_hbm[w].at[layer, pl.ds(kk * tk + c * rc, rc), pl.ds(off + jj * tn, tn)]
        return pltpu.make_async_copy(src, stage_refs[w].at[sslot], sem.at[w, sslot])

    n_i = n_chunk + (post is not None)
    t = b * n_i + i
    tslot = lax.rem(t, 2)

    def step_chunk(tt):
        return jnp.minimum(tt // n_i + 1, n_blocks - 1), jnp.minimum(lax.rem(tt, n_i), n_chunk - 1)

    def round_chunk(w, wb, c, sslot):
        wb[w][pl.ds(pl.multiple_of(c * rc, rc), rc), :] = stage_refs[w][sslot].astype(BF16)

    @pl.when(t == 0)
    def _():
        if post is not None:
            post[2](post_refs)
        for acc in acc_refs:
            acc[...] = jnp.zeros(acc.shape, F32)
        steps = (tn if nt else tk) // rc
        for w in range(n_w):
            chunk_copy(w, 0, 0, 0).start()
            for c in range(steps):
                if c + 1 < steps:
                    chunk_copy(w, 0, c + 1, (c + 1) % 2).start()
                chunk_copy(w, 0, c, c % 2).wait()
                round_chunk(w, wb_even, c, c % 2)
        for w in range(n_w):
            chunk_copy(w, *step_chunk(0), 0).start()
            chunk_copy(w, *step_chunk(1), 1).start()

    def finish(accs):
        if post is not None:
            post[1](accs, post_refs, i)
        else:
            for r in range(0, tm, EPI_ROWS):
                rows_ = slice(r, r + EPI_ROWS)
                o_ref[rows_, :] = epilogue(*[x[rows_, :] for x in accs],
                                           *[e[rows_, :] for e in epi_refs]).astype(o_ref.dtype)

    def body(wb_use, wb_fill):
        if post is not None:
            post[0](o_ref, epi_refs, post_refs, j, i)
        blk, c = step_chunk(t)
        for w in range(n_w):
            chunk_copy(w, blk, c, tslot).wait()
            round_chunk(w, wb_fill, c, tslot)
        blk, c = step_chunk(t + 2)
        for w in range(n_w):
            chunk_copy(w, blk, c, tslot).start()
        a = a_ref[...].astype(BF16)
        if nt:
            parts = [_dot_nt(a, wb[...]) for wb in wb_use]
        else:
            parts = [jnp.dot(a, wb[...], preferred_element_type=F32) for wb in wb_use]
        if nk == 1:
            finish(parts)
            return
        rows = pl.ds(pl.multiple_of(i * tm, tm), tm)
        sums = [jnp.where(k > 0, acc[rows, :], 0.0) + part for acc, part in zip(acc_refs, parts)]
        for acc, total in zip(acc_refs, sums):
            acc[rows, :] = total
        finish(sums)

    even = lax.rem(b, 2) == 0
    pl.when(even)(lambda: body(wb_even, wb_odd))
    pl.when(jnp.logical_not(even))(lambda: body(wb_odd, wb_even))

    @pl.when(t == n_blocks * n_i - 1)
    def _():
        for ahead in (1, 2):
            blk, c = step_chunk(t + ahead)
            for w in range(n_w):
                chunk_copy(w, blk, c, lax.rem(t + ahead, 2)).wait()


def _matmul(a, ws, *, n, out_dtype, tm, tn, tk, name, nt=False, epilogue=None, epi=(), post=None, post_scratch=()):
    M, K = a.shape
    nk = K // tk
    n_i = M // tm
    assert M % tm == 0 and n % tn == 0 and K % tk == 0
    rc = (tn if nt else tk) // n_i
    assert rc * n_i == (tn if nt else tk) and rc % 16 == 0
    if epilogue is None:
        epilogue = lambda acc: acc
    if post is not None:
        assert nk == 1
        a_map = lambda j, k, i: (jnp.minimum(i, n_i - 1), k)
    else:
        a_map = lambda j, k, i: (i, k)
    in_specs = [pl.BlockSpec((tm, tk), a_map)]
    args = [a]
    for w, _, _ in ws:
        in_specs.append(pl.BlockSpec(memory_space=pl.ANY))
        args.append(w)
    for e, *spec in epi:
        if len(spec) == 1:
            in_specs.append(pl.BlockSpec((tm, tn), lambda j, k, i, off=spec[0]: (i, off + j)))
        else:
            in_specs.append(pl.BlockSpec(*spec))
        args.append(e)
    wshape = (tn, tk) if nt else (tk, tn)
    scratch = [pltpu.VMEM(wshape, BF16) for _ in ws] * 2
    scratch += [pltpu.VMEM((2, rc, wshape[1]), F32) for _ in ws]
    scratch += [pltpu.SemaphoreType.DMA((len(ws), 2))]
    if nk > 1:
        scratch += [pltpu.VMEM((M, tn), F32) for _ in ws]
        out_map = lambda j, k, i: (jnp.where(k == nk - 1, i, 0), j)
    elif post is not None:
        out_map = lambda j, k, i: (jnp.maximum(i - 1, 0), j)
    else:
        out_map = lambda j, k, i: (i, j)
    scratch += list(post_scratch)
    kern = functools.partial(_mm_kernel, n_w=len(ws), n_epi=len(epi), nk=nk, n_chunk=n_i, tm=tm, tn=tn, tk=tk, rc=rc,
                             nt=nt,
                             w_index=tuple((layer, off) for _, layer, off in ws), epilogue=epilogue, post=post)
    return pl.pallas_call(
        kern,
        grid=(n // tn, nk, n_i + (post is not None)),
        in_specs=in_specs,
        out_specs=pl.BlockSpec((tm, tn), out_map),
        out_shape=jax.ShapeDtypeStruct((M, n), out_dtype),
        scratch_shapes=scratch,
        compiler_params=_cparams(("arbitrary", "arbitrary", "arbitrary")),
        name=name,
    )(*args)


def _glu_epilogue(acc_a, acc_b):
    return acc_a * _sigmoid(acc_b)


def _merge_epilogue(acc, gate_s, gate_d, y_ssm):
    return (_sigmoid(gate_s.astype(F32)) * y_ssm.astype(F32)
            + _sigmoid(gate_d.astype(F32)) * acc)


def _relu2_epilogue(acc):
    r = jnp.maximum(acc, 0.0)
    return r * r


def _dot_nt(a, b):
    return lax.dot_general(a, b, (((1,), (1,)), ((), ())), preferred_element_type=F32)


def _dot_tn(a, b):
    return lax.dot_general(a, b, (((0,), (0,)), ((), ())), preferred_element_type=F32)


def _ssm_tables(a_re, a_im, log_dt, b_re, b_im, c_re, c_im, d_skip):
    T = SSM_FOLD
    depth, G, N = a_re.shape
    H = b_re.shape[-1]
    gps = LANES // H
    n_slab = G // gps
    rows = T * gps * H
    dup = lambda t: jnp.concatenate([t, t], axis=-1)
    dt = jnp.exp(log_dt.astype(F32))[..., None]
    ar = dup(a_re.astype(F32))
    ai = dup(a_im.astype(F32))
    mag = jnp.exp(ar * dt)
    abar_r = mag * jnp.cos(ai * dt)
    abar_i = mag * jnp.sin(ai * dt)
    den = ar * ar + ai * ai
    fr = (((abar_r - 1.0) * ar + abar_i * ai) / den)[:, :, None, :]
    fi = ((abar_i * ar - (abar_r - 1.0) * ai) / den)[:, :, None, :]
    br = dup(jnp.swapaxes(b_re.astype(F32), -1, -2))
    bi = dup(jnp.swapaxes(b_im.astype(F32), -1, -2))
    bbar_r = fr * br - fi * bi
    bbar_i = fr * bi + fi * br
    p = jnp.arange(T + 1, dtype=F32)[None, :, None, None]
    pmag = jnp.exp(p * (ar * dt)[:, None])
    pow_r = (pmag * jnp.cos(p * (ai * dt)[:, None]))[:, :, :, None, :]
    pow_i = (pmag * jnp.sin(p * (ai * dt)[:, None]))[:, :, :, None, :]
    pow_r, pow_i, bbar_r, bbar_i = lax.optimization_barrier((pow_r, pow_i, bbar_r, bbar_i))
    cr = dup(c_re.astype(F32))[:, None]
    ci = dup(c_im.astype(F32))[:, None]

    def compact(tr, ti):
        t = jnp.stack([tr, ti], axis=1).astype(BF16).reshape(depth, 2, T, n_slab, gps * H, 2 * N)
        return t.transpose(0, 3, 1, 2, 4, 5).reshape(depth, n_slab, 2, rows, 2 * N)

    pw_r = pow_r[:, T - 1::-1]
    pw_i = pow_i[:, T - 1::-1]
    pc = compact(pw_r * bbar_r[:, None] - pw_i * bbar_i[:, None], pw_r * bbar_i[:, None] + pw_i * bbar_r[:, None])

    ca_r = cr * pow_r - ci * pow_i
    ca_i = cr * pow_i + ci * pow_r
    qc = compact(ca_r[:, 1:], -ca_i[:, 1:])

    kb = (jnp.einsum("dlghn,dgkn->dgklh", ca_r[:, :T, ..., :N], bbar_r[..., :N])
          - jnp.einsum("dlghn,dgkn->dgklh", ca_i[:, :T, ..., :N], bbar_i[..., :N]))
    kb = kb.reshape(depth, n_slab, gps * H, T * H).astype(BF16)
    src = np.arange(T * H)
    dst = np.arange(rows)
    spread = ((src[:, None] // H == dst[None, :] // (gps * H)) & (src[:, None] % H == dst[None, :] % H))
    spread = jnp.asarray(spread, BF16)

    a_pow = jnp.stack([pow_r[:, T, ..., :N].reshape(depth, n_slab, gps * N),
                       pow_i[:, T, ..., :N].reshape(depth, n_slab, gps * N)], axis=2)
    d_tab = jnp.tile(d_skip.astype(F32).reshape(depth, n_slab, 1, LANES), (1, 1, 1, T))
    return pc, qc, kb, spread, a_pow, d_tab


def _ssm_kernel(u_ref, p_ref, q_ref, k_ref, spread_ref, apow_ref, d_ref, o_ref, s_ref, xp_ref, *, group):
    T = SSM_FOLD
    nc = u_ref.shape[0] // T
    ns = s_ref.shape[1]
    half = ns // 2
    fw = T * LANES
    n_state = p_ref.shape[2] // 2
    gps = LANES // group
    row_g = (lax.broadcasted_iota(jnp.int32, (fw, LANES), 0) // group) % gps
    lane = lax.broadcasted_iota(jnp.int32, (fw, LANES), 1)

    def state_table(ref):
        tiles = []
        for c in range(2):
            t = ref[c]
            for v in range(half // LANES):
                tiles.append(jnp.where(row_g == v * (LANES // n_state) + lane // n_state, t, jnp.zeros_like(t)))
        return jnp.concatenate(tiles, axis=1)

    p_tab = state_table(p_ref)
    qt_tab = state_table(q_ref)
    kb = k_ref[...].astype(F32)
    klane = lax.broadcasted_iota(jnp.int32, kb.shape, 1)
    kc = jnp.concatenate(
        [kb] + [jnp.where(klane >= group * i, pltpu.roll(kb, group * i, axis=1), 0.0) for i in range(1, T)],
        axis=0).astype(BF16)
    m_rep = jnp.dot(kc, spread_ref[...], preferred_element_type=F32)
    same_g = ((lax.broadcasted_iota(jnp.int32, (fw, fw), 0) // group) % gps
              == (lax.broadcasted_iota(jnp.int32, (fw, fw), 1) // group) % gps)
    m_tab = jnp.where(same_g, m_rep, 0.0).astype(BF16)
    uf32 = jnp.concatenate([u_ref[pl.ds(i, nc, stride=T), :] for i in range(T)], axis=1)
    uf = uf32.astype(BF16)
    s_ref[...] = jnp.dot(uf, p_tab, preferred_element_type=F32)
    ar = apow_ref[0:1, :]
    ai = apow_ref[1:2, :]

    def step(c, carry):
        xr, xi = carry
        xp_ref[pl.ds(c, 1), 0:half] = xr
        xp_ref[pl.ds(c, 1), half:2 * half] = xi
        sr = s_ref[pl.ds(c, 1), 0:half]
        si = s_ref[pl.ds(c, 1), half:2 * half]
        return ar * xr - ai * xi + sr, ar * xi + ai * xr + si

    zero = jnp.zeros((1, half), F32)
    lax.fori_loop(0, nc, step, (zero, zero), unroll=8)
    y = (jnp.dot(uf, m_tab, preferred_element_type=F32)
         + _dot_nt(xp_ref[...].astype(BF16), qt_tab)
         + d_ref[...] * uf32)
    g = jax.nn.gelu(y, approximate=True)
    for j in range(T):
        o_ref[pl.ds(j, nc, stride=T), :] = g[:, j * LANES:(j + 1) * LANES]


def _ssm(u, tables, layer):
    pc, qc, kb, spread, a_pow, d_tab = tables
    L, W = u.shape
    n_slab = W // LANES
    nc = L // SSM_FOLD
    fw = SSM_FOLD * LANES
    ns = 2 * a_pow.shape[3]
    group = LANES * pc.shape[4] // ns
    tab = lambda *shape: pl.BlockSpec((None, None) + shape, lambda s: (layer, s) + (0,) * len(shape))
    return pl.pallas_call(
        functools.partial(_ssm_kernel, group=group),
        grid=(n_slab,),
        in_specs=[pl.BlockSpec((L, LANES), lambda s: (0, s)),
                  tab(2, fw, pc.shape[4]), tab(2, fw, qc.shape[4]), tab(*kb.shape[2:]),
                  pl.BlockSpec(spread.shape, lambda s: (0, 0)),
                  tab(2, ns // 2), tab(1, fw)],
        out_specs=pl.BlockSpec((L, LANES), lambda s: (0, s)),
        out_shape=jax.ShapeDtypeStruct((L, W), F32),
        scratch_shapes=[pltpu.VMEM((nc, ns), F32), pltpu.VMEM((nc, ns), F32)],
        compiler_params=_cparams(("parallel",)),
        name="s5_ssm",
    )(u, pc, qc, kb, spread, a_pow, d_tab)


def _gdn_kernel(qkv_ref, ab_ref, z_ref, alog_ref, dtb_ref, normw_ref, o_ref, s_ref, *, n_heads, dk, dv):
    C = GDN_CHUNK
    c = pl.program_id(0)

    @pl.when(c == 0)
    def _():
        s_ref[...] = jnp.zeros(s_ref.shape, F32)

    row = lax.broadcasted_iota(jnp.int32, (C, 2 * C), 0)
    col = lax.broadcasted_iota(jnp.int32, (C, 2 * C), 1)
    colm = jnp.where(col >= C, col - C, col)
    causal2 = colm <= row
    strict2 = colm < row
    low_half = col < C
    eye2 = jnp.where(colm == row, 1.0, 0.0)
    tri = jnp.where(lax.broadcasted_iota(jnp.int32, (C, C), 1) <= lax.broadcasted_iota(jnp.int32, (C, C), 0),
                    1.0, 0.0)

    ab = ab_ref[...]
    sp_in = ab + dtb_ref[...]
    softplus = jnp.maximum(sp_in, 0.0) + jnp.log(1.0 + jnp.exp(-jnp.abs(sp_in)))
    g = -jnp.exp(alog_ref[...]) * softplus
    beta = _sigmoid(ab)
    gc = jnp.dot(tri, g, preferred_element_type=F32, precision=lax.Precision.HIGHEST)
    gct = jnp.transpose(jnp.concatenate([gc, gc], axis=0))

    heads = range(n_heads)
    zero_cc = jnp.zeros((C, 2 * C), F32)
    dot = functools.partial(jnp.dot, preferred_element_type=F32)
    st = []
    for h in heads:
        qn = qkv_ref[:, h * dk:(h + 1) * dk].astype(F32)
        kn = qkv_ref[:, (n_heads + h) * dk:(n_heads + h + 1) * dk].astype(F32)
        vh = qkv_ref[:, 2 * n_heads * dk + h * dv:2 * n_heads * dk + (h + 1) * dv].astype(F32)
        gcol = gc[:, h:h + 1]
        grow2 = gct[h:h + 1, :]
        bcol = beta[:, n_heads + h:n_heads + h + 1]
        glast = gc[C - 1:C, h:h + 1]
        egc = jnp.exp(gcol)
        kb = kn * bcol
        r = jnp.concatenate([vh * bcol, kb * egc], axis=1)
        st.append(dict(
            decay2=jnp.exp(jnp.where(causal2, gcol - grow2, -jnp.inf)),
            k2=jnp.concatenate([kn, kn], axis=0).astype(BF16),
            kbq=jnp.concatenate([kb, qn], axis=0).astype(BF16),
            r2=jnp.concatenate([r, jnp.zeros_like(r)], axis=0).astype(BF16),
            qg=qn * egc,
            kd=(kn * jnp.exp(glast - gcol)).astype(BF16),
            eg=jnp.exp(glast)))
    top = lambda x: jnp.concatenate([x, zero_cc], axis=0).astype(BF16)
    sc = [_dot_nt(t["kbq"], t["k2"]) for t in st]
    n2 = [jnp.where(strict2, s_[0:C] * t["decay2"], 0.0) for s_, t in zip(sc, st)]
    qk2 = [jnp.where(causal2 & low_half, s_[C:2 * C] * t["decay2"], 0.0).astype(BF16) for s_, t in zip(sc, st)]
    valid = low_half & strict2
    blk = 2
    w = [jnp.where(low_half, eye2, 0.0) - jnp.where(valid & (row // blk == colm // blk), n, 0.0) for n in n2]
    while blk < C:
        off = valid & (row // (2 * blk) == colm // (2 * blk)) & (row // blk > colm // blk)
        y = [dot(jnp.where(off, n, 0.0).astype(BF16), top(x)) for n, x in zip(n2, w)]
        w = [x - dot(x.astype(BF16), top(y_)) for x, y_ in zip(w, y)]
        blk *= 2
    uw = [dot(x.astype(BF16), t["r2"]) for x, t in zip(w, st)]
    s_old = [s_ref[h] for h in heads]
    ws = [dot(jnp.concatenate([x[:, dv:dv + dk], t["qg"]], axis=0).astype(BF16), s_.astype(BF16))
          for x, t, s_ in zip(uw, st, s_old)]
    vn = [x[:, 0:dv] - y[0:C] for x, y in zip(uw, ws)]
    o = [y[C:2 * C] + dot(q_, jnp.concatenate([v_, v_], axis=0).astype(BF16)) for y, q_, v_ in zip(ws, qk2, vn)]
    s_new = [s_ * t["eg"] + _dot_tn(t["kd"], v_.astype(BF16)) for s_, t, v_ in zip(s_old, st, vn)]
    for h in heads:
        s_ref[h] = s_new[h]
        zh = z_ref[:, h * dv:(h + 1) * dv].astype(F32)
        on = _rms(o[h], normw_ref[...]) * (zh * _sigmoid(zh))
        o_ref[:, h * dv:(h + 1) * dv] = on.astype(o_ref.dtype)


def _conv_stash(accs, post_refs, i):
    (buf,) = post_refs
    buf[lax.rem(i, 2), CONV_HALO:, :] = accs[0]


def _conv_silu_norm_finish(o_ref, epi_refs, post_refs, j, i, *, tm, dk, n_q, n_qk, q_scale):
    (cw_ref,) = epi_refs
    (buf,) = post_refs
    kw = cw_ref.shape[0]
    halo = CONV_HALO
    cur = lax.rem(i, 2)
    prev = 1 - cur
    buf[cur, 0:halo, :] = jnp.where(i > 0, buf[prev, tm:tm + halo, :], 0.0)
    scale = jnp.where(j < n_q, q_scale, 1.0)
    for s in range(buf.shape[2] // dk):
        lanes = slice(s * dk, (s + 1) * dk)
        taps = [cw_ref[t:t + 1, lanes] for t in range(kw)]
        for r in range(0, tm, EPI_ROWS):
            ext = buf[prev, r:r + halo + EPI_ROWS, lanes]
            y = ext[halo:, :] * taps[kw - 1]
            for d in range(1, kw):
                y = y + pltpu.roll(ext, d, axis=0)[halo:, :] * taps[kw - 1 - d]
            x = y * _sigmoid(y)
            inv = lax.rsqrt(jnp.sum(x * x, axis=-1, keepdims=True) + EPS) * scale
            inv = jnp.where(j < n_qk, inv, 1.0)
            o_ref[r:r + EPI_ROWS, lanes] = (x * inv).astype(o_ref.dtype)


def _conv_init(post_refs):
    (buf,) = post_refs
    buf[...] = jnp.zeros(buf.shape, F32)


def _gdn(qkv, ab, zg, a_log, dt_bias, norm_w, *, n_heads, dk, dv):
    L, cw = qkv.shape
    C = GDN_CHUNK
    pad = lambda v: jnp.pad(v.astype(F32), (0, LANES - v.shape[0])).reshape(1, LANES)
    kern = functools.partial(_gdn_kernel, n_heads=n_heads, dk=dk, dv=dv)
    return pl.pallas_call(
        kern,
        grid=(L // C,),
        in_specs=[pl.BlockSpec((C, cw), lambda c: (c, 0)),
                  pl.BlockSpec((C, LANES), lambda c: (c, 0)),
                  pl.BlockSpec((C, n_heads * dv), lambda c: (c, 0)),
                  pl.BlockSpec((1, LANES), lambda c: (0, 0)),
                  pl.BlockSpec((1, LANES), lambda c: (0, 0)),
                  pl.BlockSpec((1, dv), lambda c: (0, 0))],
        out_specs=pl.BlockSpec((C, n_heads * dv), lambda c: (c, 0)),
        out_shape=jax.ShapeDtypeStruct((L, n_heads * dv), BF16),
        scratch_shapes=[pltpu.VMEM((n_heads, dk, dv), F32)],
        compiler_params=_cparams(("arbitrary",)),
        name="gated_deltanet",
    )(qkv, ab, zg, pad(a_log), pad(dt_bias), norm_w.astype(F32).reshape(1, dv))


def kernel(x, w_in, ssm_a_re, ssm_a_im, ssm_log_dt, ssm_b_re, ssm_b_im, ssm_c_re, ssm_c_im, ssm_d, w_glu, conv_w, gdn_a_log, gdn_dt_bias, gdn_norm_w, w_gdn_out, w_out, mix_pre_w, mix_post_w, ffn_pre_w, ffn_post_w, w_ff1, w_ff2):
    depth = w_in.shape[0]
    B, L, D = x.shape
    assert B == 1
    n_heads = gdn_a_log.shape[1]
    dv = gdn_norm_w.shape[1]
    gdn_val = w_gdn_out.shape[1]
    assert gdn_val == n_heads * dv
    conv_ch = conv_w.shape[2]
    gdn_key = (conv_ch - gdn_val) // 2
    dk = gdn_key // n_heads
    ssm_w = ssm_d.shape[1]
    d_ff = w_ff1.shape[2]
    o_u = 0
    o_qkv = o_u + ssm_w
    o_a = o_qkv + conv_ch
    o_b = o_a + n_heads
    o_z = o_b + n_heads
    o_gs = o_z + gdn_val
    o_gd = o_gs + D
    assert o_gd + D == w_in.shape[2]
    assert 2 * n_heads <= LANES
    tm = min(MM_TM, L)
    mm = functools.partial(_matmul, tm=tm, tn=MM_TN)
    w_in_t = jnp.swapaxes(w_in, 1, 2)

    xs = x.reshape(L, D)
    h = _prenorm(xs, mix_pre_w[0])
    tables = _ssm_tables(ssm_a_re, ssm_a_im, ssm_log_dt, ssm_b_re, ssm_b_im, ssm_c_re, ssm_c_im, ssm_d)
    for i in range(depth):
        u = mm(h, [(w_in_t, i, o_u)], n=ssm_w, out_dtype=F32, tk=D, nt=True, name="in_proj_u")
        tm_c = min(MM_TM_CONV, L)
        finish_prev = functools.partial(_conv_silu_norm_finish, tm=tm_c, dk=dk, n_q=gdn_key // MM_TN,
                                        n_qk=2 * gdn_key // MM_TN, q_scale=dk ** -0.5)
        qkv = _matmul(h, [(w_in_t, i, o_qkv)], n=conv_ch, out_dtype=BF16, tm=tm_c, tn=MM_TN, tk=D, nt=True,
                      name="in_proj_qkv",
                      epi=[(conv_w[i].astype(F32), (conv_w.shape[1], MM_TN), lambda j, k, r: (0, j))],
                      post=(finish_prev, _conv_stash, _conv_init),
                      post_scratch=[pltpu.VMEM((2, CONV_HALO + tm_c, MM_TN), F32)])
        ab = _matmul(h, [(w_in_t, i, o_a)], n=LANES, out_dtype=F32, tm=tm, tn=LANES, tk=D, nt=True,
                     name="in_proj_ab")
        zg = mm(h, [(w_in_t, i, o_z)], n=gdn_val + 2 * D, out_dtype=BF16, tk=D, nt=True,
                name="in_proj_zg")

        y = _ssm(u, tables, i)
        y_ssm = mm(y, [(w_glu, i, 0), (w_glu, i, D)], n=D, out_dtype=BF16, tk=ssm_w,
                   name="glu", epilogue=_glu_epilogue)

        o = _gdn(qkv, ab, zg, gdn_a_log[i], gdn_dt_bias[i], gdn_norm_w[i], n_heads=n_heads, dk=dk, dv=dv)
        merged = mm(o, [(w_gdn_out, i, 0)], n=D, out_dtype=BF16, tk=gdn_val,
                    name="gdn_out_merge", epilogue=_merge_epilogue,
                    epi=[(zg, gdn_val // MM_TN), (zg, (gdn_val + D) // MM_TN), (y_ssm, 0)])
        hm = mm(merged, [(w_out, i, 0)], n=D, out_dtype=BF16, tk=D, name="out_proj")
        xs, hn = _resnorm(hm, xs, mix_post_w[i], ffn_pre_w[i])

        f1 = mm(hn, [(w_ff1, i, 0)], n=d_ff, out_dtype=BF16, tk=D, name="ff1", epilogue=_relu2_epilogue)
        f2 = _matmul(f1, [(w_ff2, i, 0)], n=D, out_dtype=BF16, tm=tm, tn=MM_TN_FF2, tk=MM_TK_FF2, name="ff2")
        xs, h = _resnorm(f2, xs, ffn_post_w[i], mix_pre_w[i + 1] if i + 1 < depth else None)
    return xs.reshape(B, L, D)
```

```python
import functools
import math

import jax
import jax.numpy as jnp
import numpy as np
from jax import lax
from jax.experimental import pallas as pl
from jax.experimental.pallas import tpu as pltpu

F32 = jnp.float32
BF16 = jnp.bfloat16
EPS = 1e-6

V7X_VMEM_LIMIT_BYTES = 56 * 1024 * 1024
LANES = 128
SSM_FOLD = 8
GDN_CHUNK = 64
CONV_HALO = 8
EPI_ROWS = 16
MM_TM = 1024
MM_TN = 1024
MM_TN_FF2 = 512
MM_TK_FF2 = 4096


def _cparams(sem):
    return pltpu.CompilerParams(dimension_semantics=sem, vmem_limit_bytes=V7X_VMEM_LIMIT_BYTES)


def _rms(xf, w):
    return xf * lax.rsqrt(jnp.mean(xf * xf, axis=-1, keepdims=True) + EPS) * w


def _sigmoid(x):
    return 1.0 / (1.0 + jnp.exp(-x))


def _prenorm_kernel(x_ref, w_ref, o_ref):
    o_ref[...] = _rms(x_ref[...], w_ref[...]).astype(o_ref.dtype)


def _prenorm(x, w, *, tr=256):
    L, D = x.shape
    return pl.pallas_call(
        _prenorm_kernel,
        grid=(L // tr,),
        in_specs=[pl.BlockSpec((tr, D), lambda i: (i, 0)), pl.BlockSpec((1, D), lambda i: (0, 0))],
        out_specs=pl.BlockSpec((tr, D), lambda i: (i, 0)),
        out_shape=jax.ShapeDtypeStruct((L, D), BF16),
        compiler_params=_cparams(("parallel",)),
        name="prenorm",
    )(x, w.reshape(1, D))


def _resnorm_kernel(h_ref, x_ref, wpost_ref, wpre_ref, xo_ref, ho_ref):
    xn = x_ref[...] + _rms(h_ref[...].astype(F32), wpost_ref[...])
    xo_ref[...] = xn
    ho_ref[...] = _rms(xn, wpre_ref[...]).astype(ho_ref.dtype)


def _resnorm_last_kernel(h_ref, x_ref, wpost_ref, xo_ref):
    xo_ref[...] = x_ref[...] + _rms(h_ref[...].astype(F32), wpost_ref[...])


def _resnorm(h, x, w_post, w_pre, *, tr=256):
    L, D = x.shape
    row = pl.BlockSpec((tr, D), lambda i: (i, 0))
    vec = pl.BlockSpec((1, D), lambda i: (0, 0))
    if w_pre is None:
        return pl.pallas_call(
            _resnorm_last_kernel, grid=(L // tr,), in_specs=[row, row, vec], out_specs=row,
            out_shape=jax.ShapeDtypeStruct((L, D), F32),
            compiler_params=_cparams(("parallel",)), name="resnorm_last",
        )(h, x, w_post.reshape(1, D)), None
    return pl.pallas_call(
        _resnorm_kernel, grid=(L // tr,), in_specs=[row, row, vec, vec], out_specs=[row, row],
        out_shape=[jax.ShapeDtypeStruct((L, D), F32), jax.ShapeDtypeStruct((L, D), BF16)],
        compiler_params=_cparams(("parallel",)), name="resnorm",
    )(h, x, w_post.reshape(1, D), w_pre.reshape(1, D))


def _mm_kernel(*refs, n_w, n_epi, nk, n_chunk, tm, tn, tk, rc, nt, w_index, epilogue, post):
    a_ref = refs[0]
    w_hbm = refs[1:1 + n_w]
    epi_refs = refs[1 + n_w:1 + n_w + n_epi]
    o_ref = refs[1 + n_w + n_epi]
    scratch = refs[2 + n_w + n_epi:]
    wb_even = scratch[:n_w]
    wb_odd = scratch[n_w:2 * n_w]
    stage_refs = scratch[2 * n_w:3 * n_w]
    sem = scratch[3 * n_w]
    n_acc = n_w if nk > 1 else 0
    acc_refs = scratch[3 * n_w + 1:3 * n_w + 1 + n_acc]
    post_refs = scratch[3 * n_w + 1 + n_acc:]
    j = pl.program_id(0)
    k = pl.program_id(1)
    i = pl.program_id(2)
    b = j * nk + k
    n_blocks = pl.num_programs(0) * nk

    def chunk_copy(w, blk, c, sslot):
        jj, kk = (blk, 0) if nk == 1 else (blk // nk, lax.rem(blk, nk))
        layer, off = w_index[w]
        if nt:
            src = w_hbm[w].at[layer, pl.ds(off + jj * tn + c * rc, rc), pl.ds(kk * tk, tk)]
        else:
            src = w_hbm[w].at[layer, pl.ds(kk * tk + c * rc, rc), pl.ds(off + jj * tn, tn)]
        return pltpu.make_async_copy(src, stage_refs[w].at[sslot], sem.at[w, sslot])

    n_i = n_chunk
    t = b * n_i + i
    tslot = lax.rem(t, 2)

    def step_chunk(tt):
        return jnp.minimum(tt // n_i + 1, n_blocks - 1), jnp.minimum(lax.rem(tt, n_i), n_chunk - 1)

    def round_chunk(w, wb, c, sslot):
        wb[w][pl.ds(pl.multiple_of(c * rc, rc), rc), :] = stage_refs[w][sslot].astype(BF16)

    @pl.when(t == 0)
    def _():
        if post is not None:
            post[1](post_refs)
        for acc in acc_refs:
            acc[...] = jnp.zeros(acc.shape, F32)
        steps = (tn if nt else tk) // rc
        for w in range(n_w):
            chunk_copy(w, 0, 0, 0).start()
            for c in range(steps):
                if c + 1 < steps:
                    chunk_copy(w, 0, c + 1, (c + 1) % 2).start()
                chunk_copy(w, 0, c, c % 2).wait()
                round_chunk(w, wb_even, c, c % 2)
        for w in range(n_w):
            chunk_copy(w, *step_chunk(0), 0).start()
            chunk_copy(w, *step_chunk(1), 1).start()

    def finish(accs):
        if post is not None:
            post[0](o_ref, accs, epi_refs, post_refs, j, i)
        else:
            for r in range(0, tm, EPI_ROWS):
                rows_ = slice(r, r + EPI_ROWS)
                o_ref[rows_, :] = epilogue(*[x[rows_, :] for x in accs],
                                           *[e[rows_, :] for e in epi_refs]).astype(o_ref.dtype)

    def body(wb_use, wb_fill):
        blk, c = step_chunk(t)
        for w in range(n_w):
            chunk_copy(w, blk, c, tslot).wait()
            round_chunk(w, wb_fill, c, tslot)
        blk, c = step_chunk(t + 2)
        for w in range(n_w):
            chunk_copy(w, blk, c, tslot).start()
        a = a_ref[...].astype(BF16)
        if nt:
            parts = [_dot_nt(a, wb[...]) for wb in wb_use]
        else:
            parts = [jnp.dot(a, wb[...], preferred_element_type=F32) for wb in wb_use]
        if nk == 1:
            finish(parts)
            return
        rows = pl.ds(pl.multiple_of(i * tm, tm), tm)
        sums = [jnp.where(k > 0, acc[rows, :], 0.0) + part for acc, part in zip(acc_refs, parts)]
        for acc, total in zip(acc_refs, sums):
            acc[rows, :] = total
        finish(sums)

    even = lax.rem(b, 2) == 0
    pl.when(even)(lambda: body(wb_even, wb_odd))
    pl.when(jnp.logical_not(even))(lambda: body(wb_odd, wb_even))

    @pl.when(t == n_blocks * n_i - 1)
    def _():
        for ahead in (1, 2):
            blk, c = step_chunk(t + ahead)
            for w in range(n_w):
                chunk_copy(w, blk, c, lax.rem(t + ahead, 2)).wait()


def _matmul(a, ws, *, n, out_dtype, tm, tn, tk, name, nt=False, epilogue=None, epi=(), post=None, post_scratch=()):
    M, K = a.shape
    nk = K // tk
    n_i = M // tm
    assert M % tm == 0 and n % tn == 0 and K % tk == 0
    rc = (tn if nt else tk) // n_i
    assert rc * n_i == (tn if nt else tk) and rc % 16 == 0
    if epilogue is None:
        epilogue = lambda acc: acc
    in_specs = [pl.BlockSpec((tm, tk), lambda j, k, i: (i, k))]
    args = [a]
    for w, _, _ in ws:
        in_specs.append(pl.BlockSpec(memory_space=pl.ANY))
        args.append(w)
    for e, *spec in epi:
        if len(spec) == 1:
            in_specs.append(pl.BlockSpec((tm, tn), lambda j, k, i, off=spec[0]: (i, off + j)))
        else:
            in_specs.append(pl.BlockSpec(*spec))
        args.append(e)
    wshape = (tn, tk) if nt else (tk, tn)
    scratch = [pltpu.VMEM(wshape, BF16) for _ in ws] * 2
    scratch += [pltpu.VMEM((2, rc, wshape[1]), F32) for _ in ws]
    scratch += [pltpu.SemaphoreType.DMA((len(ws), 2))]
    if nk > 1:
        scratch += [pltpu.VMEM((M, tn), F32) for _ in ws]
        out_map = lambda j, k, i: (jnp.where(k == nk - 1, i, 0), j)
    else:
        out_map = lambda j, k, i: (i, j)
    scratch += list(post_scratch)
    kern = functools.partial(_mm_kernel, n_w=len(ws), n_epi=len(epi), nk=nk, n_chunk=n_i, tm=tm, tn=tn, tk=tk, rc=rc,
                             nt=nt,
                             w_index=tuple((layer, off) for _, layer, off in ws), epilogue=epilogue, post=post)
    return pl.pallas_call(
        kern,
        grid=(n // tn, nk, n_i),
        in_specs=in_specs,
        out_specs=pl.BlockSpec((tm, tn), out_map),
        out_shape=jax.ShapeDtypeStruct((M, n), out_dtype),
        scratch_shapes=scratch,
        compiler_params=_cparams(("arbitrary", "arbitrary", "arbitrary")),
        name=name,
    )(*args)


def _glu_epilogue(acc_a, acc_b):
    return acc_a * _sigmoid(acc_b)


def _merge_epilogue(acc, gate_s, gate_d, y_ssm):
    return (_sigmoid(gate_s.astype(F32)) * y_ssm.astype(F32)
            + _sigmoid(gate_d.astype(F32)) * acc)


def _relu2_epilogue(acc):
    r = jnp.maximum(acc, 0.0)
    return r * r


def _dot_nt(a, b):
    return lax.dot_general(a, b, (((1,), (1,)), ((), ())), preferred_element_type=F32)


def _dot_tn(a, b):
    return lax.dot_general(a, b, (((0,), (0,)), ((), ())), preferred_element_type=F32)


def _ssm_tables(a_re, a_im, log_dt, b_re, b_im, c_re, c_im, d_skip):
    T = SSM_FOLD
    depth, G, N = a_re.shape
    H = b_re.shape[-1]
    gps = LANES // H
    n_slab = G // gps
    rows = T * gps * H
    dup = lambda t: jnp.concatenate([t, t], axis=-1)
    dt = jnp.exp(log_dt.astype(F32))[..., None]
    ar = dup(a_re.astype(F32))
    ai = dup(a_im.astype(F32))
    mag = jnp.exp(ar * dt)
    abar_r = mag * jnp.cos(ai * dt)
    abar_i = mag * jnp.sin(ai * dt)
    den = ar * ar + ai * ai
    fr = (((abar_r - 1.0) * ar + abar_i * ai) / den)[:, :, None, :]
    fi = ((abar_i * ar - (abar_r - 1.0) * ai) / den)[:, :, None, :]
    br = dup(jnp.swapaxes(b_re.astype(F32), -1, -2))
    bi = dup(jnp.swapaxes(b_im.astype(F32), -1, -2))
    bbar_r = fr * br - fi * bi
    bbar_i = fr * bi + fi * br
    p = jnp.arange(T + 1, dtype=F32)[None, :, None, None]
    pmag = jnp.exp(p * (ar * dt)[:, None])
    pow_r = (pmag * jnp.cos(p * (ai * dt)[:, None]))[:, :, :, None, :]
    pow_i = (pmag * jnp.sin(p * (ai * dt)[:, None]))[:, :, :, None, :]
    pow_r, pow_i, bbar_r, bbar_i = lax.optimization_barrier((pow_r, pow_i, bbar_r, bbar_i))
    cr = dup(c_re.astype(F32))[:, None]
    ci = dup(c_im.astype(F32))[:, None]

    def compact(tr, ti):
        t = jnp.stack([tr, ti], axis=1).astype(BF16).reshape(depth, 2, T, n_slab, gps * H, 2 * N)
        return t.transpose(0, 3, 1, 2, 4, 5).reshape(depth, n_slab, 2, rows, 2 * N)

    pw_r = pow_r[:, T - 1::-1]
    pw_i = pow_i[:, T - 1::-1]
    pc = compact(pw_r * bbar_r[:, None] - pw_i * bbar_i[:, None], pw_r * bbar_i[:, None] + pw_i * bbar_r[:, None])

    ca_r = cr * pow_r - ci * pow_i
    ca_i = cr * pow_i + ci * pow_r
    qc = compact(ca_r[:, 1:], -ca_i[:, 1:])

    kb = (jnp.einsum("dlghn,dgkn->dgklh", ca_r[:, :T, ..., :N], bbar_r[..., :N])
          - jnp.einsum("dlghn,dgkn->dgklh", ca_i[:, :T, ..., :N], bbar_i[..., :N]))
    kb = kb.reshape(depth, n_slab, gps * H, T * H).astype(BF16)
    src = np.arange(T * H)
    dst = np.arange(rows)
    spread = ((src[:, None] // H == dst[None, :] // (gps * H)) & (src[:, None] % H == dst[None, :] % H))
    spread = jnp.asarray(spread, BF16)

    a_pow = jnp.stack([pow_r[:, T, ..., :N].reshape(depth, n_slab, gps * N),
                       pow_i[:, T, ..., :N].reshape(depth, n_slab, gps * N)], axis=2)
    d_tab = jnp.tile(d_skip.astype(F32).reshape(depth, n_slab, 1, LANES), (1, 1, 1, T))
    return pc, qc, kb, spread, a_pow, d_tab


def _ssm_kernel(u_ref, p_ref, q_ref, k_ref, spread_ref, apow_ref, d_ref, o_ref, s_ref, xp_ref, *, group):
    T = SSM_FOLD
    nc = u_ref.shape[0] // T
    ns = s_ref.shape[1]
    half = ns // 2
    fw = T * LANES
    n_state = p_ref.shape[2] // 2
    gps = LANES // group
    row_g = (lax.broadcasted_iota(jnp.int32, (fw, LANES), 0) // group) % gps
    lane = lax.broadcasted_iota(jnp.int32, (fw, LANES), 1)

    def state_table(ref):
        tiles = []
        for c in range(2):
            t = ref[c]
            for v in range(half // LANES):
                tiles.append(jnp.where(row_g == v * (LANES // n_state) + lane // n_state, t, jnp.zeros_like(t)))
        return jnp.concatenate(tiles, axis=1)

    p_tab = state_table(p_ref)
    qt_tab = state_table(q_ref)
    kb = k_ref[...].astype(F32)
    klane = lax.broadcasted_iota(jnp.int32, kb.shape, 1)
    kc = jnp.concatenate(
        [kb] + [jnp.where(klane >= group * i, pltpu.roll(kb, group * i, axis=1), 0.0) for i in range(1, T)],
        axis=0).astype(BF16)
    m_rep = jnp.dot(kc, spread_ref[...], preferred_element_type=F32)
    same_g = ((lax.broadcasted_iota(jnp.int32, (fw, fw), 0) // group) % gps
              == (lax.broadcasted_iota(jnp.int32, (fw, fw), 1) // group) % gps)
    m_tab = jnp.where(same_g, m_rep, 0.0).astype(BF16)
    uf32 = jnp.concatenate([u_ref[pl.ds(i, nc, stride=T), :] for i in range(T)], axis=1)
    uf = uf32.astype(BF16)
    s_ref[...] = jnp.dot(uf, p_tab, preferred_element_type=F32)
    ar = apow_ref[0:1, :]
    ai = apow_ref[1:2, :]

    def step(c, carry):
        xr, xi = carry
        xp_ref[pl.ds(c, 1), 0:half] = xr
        xp_ref[pl.ds(c, 1), half:2 * half] = xi
        sr = s_ref[pl.ds(c, 1), 0:half]
        si = s_ref[pl.ds(c, 1), half:2 * half]
        return ar * xr - ai * xi + sr, ar * xi + ai * xr + si

    zero = jnp.zeros((1, half), F32)
    lax.fori_loop(0, nc, step, (zero, zero), unroll=8)
    y = (jnp.dot(uf, m_tab, preferred_element_type=F32)
         + _dot_nt(xp_ref[...].astype(BF16), qt_tab)
         + d_ref[...] * uf32)
    g = jax.nn.gelu(y, approximate=True)
    for j in range(T):
        o_ref[pl.ds(j, nc, stride=T), :] = g[:, j * LANES:(j + 1) * LANES]


def _ssm(u, tables, layer):
    pc, qc, kb, spread, a_pow, d_tab = tables
    L, W = u.shape
    n_slab = W // LANES
    nc = L // SSM_FOLD
    fw = SSM_FOLD * LANES
    ns = 2 * a_pow.shape[3]
    group = LANES * pc.shape[4] // ns
    tab = lambda *shape: pl.BlockSpec((None, None) + shape, lambda s: (layer, s) + (0,) * len(shape))
    return pl.pallas_call(
        functools.partial(_ssm_kernel, group=group),
        grid=(n_slab,),
        in_specs=[pl.BlockSpec((L, LANES), lambda s: (0, s)),
                  tab(2, fw, pc.shape[4]), tab(2, fw, qc.shape[4]), tab(*kb.shape[2:]),
                  pl.BlockSpec(spread.shape, lambda s: (0, 0)),
                  tab(2, ns // 2), tab(1, fw)],
        out_specs=pl.BlockSpec((L, LANES), lambda s: (0, s)),
        out_shape=jax.ShapeDtypeStruct((L, W), F32),
        scratch_shapes=[pltpu.VMEM((nc, ns), F32), pltpu.VMEM((nc, ns), F32)],
        compiler_params=_cparams(("parallel",)),
        name="s5_ssm",
    )(u, pc, qc, kb, spread, a_pow, d_tab)


def _gdn_kernel(qkv_ref, ab_ref, z_ref, alog_ref, dtb_ref, normw_ref, o_ref, s_ref, *, n_heads, dk, dv):
    C = GDN_CHUNK
    c = pl.program_id(0)

    @pl.when(c == 0)
    def _():
        s_ref[...] = jnp.zeros(s_ref.shape, F32)

    row = lax.broadcasted_iota(jnp.int32, (C, 2 * C), 0)
    col = lax.broadcasted_iota(jnp.int32, (C, 2 * C), 1)
    colm = jnp.where(col >= C, col - C, col)
    causal2 = colm <= row
    strict2 = colm < row
    low_half = col < C
    eye2 = jnp.where(colm == row, 1.0, 0.0)
    tri = jnp.where(lax.broadcasted_iota(jnp.int32, (C, C), 1) <= lax.broadcasted_iota(jnp.int32, (C, C), 0),
                    1.0, 0.0)

    ab = ab_ref[...]
    sp_in = ab + dtb_ref[...]
    softplus = jnp.maximum(sp_in, 0.0) + jnp.log(1.0 + jnp.exp(-jnp.abs(sp_in)))
    g = -jnp.exp(alog_ref[...]) * softplus
    beta = _sigmoid(ab)
    gc = jnp.dot(tri, g, preferred_element_type=F32, precision=lax.Precision.HIGHEST)
    gct = jnp.transpose(jnp.concatenate([gc, gc], axis=0))

    heads = range(n_heads)
    zero_cc = jnp.zeros((C, 2 * C), F32)
    dot = functools.partial(jnp.dot, preferred_element_type=F32)
    st = []
    for h in heads:
        qn = qkv_ref[:, h * dk:(h + 1) * dk].astype(F32)
        kn = qkv_ref[:, (n_heads + h) * dk:(n_heads + h + 1) * dk].astype(F32)
        vh = qkv_ref[:, 2 * n_heads * dk + h * dv:2 * n_heads * dk + (h + 1) * dv].astype(F32)
        gcol = gc[:, h:h + 1]
        grow2 = gct[h:h + 1, :]
        bcol = beta[:, n_heads + h:n_heads + h + 1]
        glast = gc[C - 1:C, h:h + 1]
        egc = jnp.exp(gcol)
        kb = kn * bcol
        r = jnp.concatenate([vh * bcol, kb * egc], axis=1)
        st.append(dict(
            decay2=jnp.exp(jnp.where(causal2, gcol - grow2, -jnp.inf)),
            k2=jnp.concatenate([kn, kn], axis=0).astype(BF16),
            kbq=jnp.concatenate([kb, qn], axis=0).astype(BF16),
            r2=jnp.concatenate([r, jnp.zeros_like(r)], axis=0).astype(BF16),
            qg=qn * egc,
            kd=(kn * jnp.exp(glast - gcol)).astype(BF16),
            eg=jnp.exp(glast)))
    top = lambda x: jnp.concatenate([x, zero_cc], axis=0).astype(BF16)
    sc = [_dot_nt(t["kbq"], t["k2"]) for t in st]
    n2 = [jnp.where(strict2, s_[0:C] * t["decay2"], 0.0) for s_, t in zip(sc, st)]
    qk2 = [jnp.where(causal2 & low_half, s_[C:2 * C] * t["decay2"], 0.0).astype(BF16) for s_, t in zip(sc, st)]
    valid = low_half & strict2
    blk = 2
    w = [jnp.where(low_half, eye2, 0.0) - jnp.where(valid & (row // blk == colm // blk), n, 0.0) for n in n2]
    while blk < C:
        off = valid & (row // (2 * blk) == colm // (2 * blk)) & (row // blk > colm // blk)
        y = [dot(jnp.where(off, n, 0.0).astype(BF16), top(x)) for n, x in zip(n2, w)]
        w = [x - dot(x.astype(BF16), top(y_)) for x, y_ in zip(w, y)]
        blk *= 2
    uw = [dot(x.astype(BF16), t["r2"]) for x, t in zip(w, st)]
    s_old = [s_ref[h] for h in heads]
    ws = [dot(jnp.concatenate([x[:, dv:dv + dk], t["qg"]], axis=0).astype(BF16), s_.astype(BF16))
          for x, t, s_ in zip(uw, st, s_old)]
    vn = [x[:, 0:dv] - y[0:C] for x, y in zip(uw, ws)]
    o = [y[C:2 * C] + dot(q_, jnp.concatenate([v_, v_], axis=0).astype(BF16)) for y, q_, v_ in zip(ws, qk2, vn)]
    s_new = [s_ * t["eg"] + _dot_tn(t["kd"], v_.astype(BF16)) for s_, t, v_ in zip(s_old, st, vn)]
    for h in heads:
        s_ref[h] = s_new[h]
        zh = z_ref[:, h * dv:(h + 1) * dv].astype(F32)
        on = _rms(o[h], normw_ref[...]) * (zh * _sigmoid(zh))
        o_ref[:, h * dv:(h + 1) * dv] = on.astype(o_ref.dtype)


def _conv_silu_norm_store(o_ref, accs, epi_refs, post_refs, j, i, *, tm, dk, n_q, n_qk, q_scale):
    (cw_ref,) = epi_refs
    (buf,) = post_refs
    kw = cw_ref.shape[0]
    halo = CONV_HALO
    buf[0:halo, :] = jnp.where(i > 0, buf[tm:tm + halo, :], 0.0)
    buf[halo:, :] = accs[0]
    scale = jnp.where(j < n_q, q_scale, 1.0)
    for s in range(buf.shape[1] // dk):
        lanes = slice(s * dk, (s + 1) * dk)
        taps = [cw_ref[t:t + 1, lanes] for t in range(kw)]
        for r in range(0, tm, EPI_ROWS):
            ext = buf[r:r + halo + EPI_ROWS, lanes]
            y = ext[halo:, :] * taps[kw - 1]
            for d in range(1, kw):
                y = y + pltpu.roll(ext, d, axis=0)[halo:, :] * taps[kw - 1 - d]
            x = y * _sigmoid(y)
            inv = lax.rsqrt(jnp.sum(x * x, axis=-1, keepdims=True) + EPS) * scale
            inv = jnp.where(j < n_qk, inv, 1.0)
            o_ref[r:r + EPI_ROWS, lanes] = (x * inv).astype(o_ref.dtype)


def _conv_init(post_refs):
    (buf,) = post_refs
    buf[...] = jnp.zeros(buf.shape, F32)


def _gdn(qkv, ab, zg, a_log, dt_bias, norm_w, *, n_heads, dk, dv):
    L, cw = qkv.shape
    C = GDN_CHUNK
    pad = lambda v: jnp.pad(v.astype(F32), (0, LANES - v.shape[0])).reshape(1, LANES)
    kern = functools.partial(_gdn_kernel, n_heads=n_heads, dk=dk, dv=dv)
    return pl.pallas_call(
        kern,
        grid=(L // C,),
        in_specs=[pl.BlockSpec((C, cw), lambda c: (c, 0)),
                  pl.BlockSpec((C, LANES), lambda c: (c, 0)),
                  pl.BlockSpec((C, n_heads * dv), lambda c: (c, 0)),
                  pl.BlockSpec((1, LANES), lambda c: (0, 0)),
                  pl.BlockSpec((1, LANES), lambda c: (0, 0)),
                  pl.BlockSpec((1, dv), lambda c: (0, 0))],
        out_specs=pl.BlockSpec((C, n_heads * dv), lambda c: (c, 0)),
        out_shape=jax.ShapeDtypeStruct((L, n_heads * dv), BF16),
        scratch_shapes=[pltpu.VMEM((n_heads, dk, dv), F32)],
        compiler_params=_cparams(("arbitrary",)),
        name="gated_deltanet",
    )(qkv, ab, zg, pad(a_log), pad(dt_bias), norm_w.astype(F32).reshape(1, dv))


def kernel(x, w_in, ssm_a_re, ssm_a_im, ssm_log_dt, ssm_b_re, ssm_b_im, ssm_c_re, ssm_c_im, ssm_d, w_glu, conv_w, gdn_a_log, gdn_dt_bias, gdn_norm_w, w_gdn_out, w_out, mix_pre_w, mix_post_w, ffn_pre_w, ffn_post_w, w_ff1, w_ff2):
    depth = w_in.shape[0]
    B, L, D = x.shape
    assert B == 1
    n_heads = gdn_a_log.shape[1]
    dv = gdn_norm_w.shape[1]
    gdn_val = w_gdn_out.shape[1]
    assert gdn_val == n_heads * dv
    conv_ch = conv_w.shape[2]
    gdn_key = (conv_ch - gdn_val) // 2
    dk = gdn_key // n_heads
    ssm_w = ssm_d.shape[1]
    d_ff = w_ff1.shape[2]
    o_u = 0
    o_qkv = o_u + ssm_w
    o_a = o_qkv + conv_ch
    o_b = o_a + n_heads
    o_z = o_b + n_heads
    o_gs = o_z + gdn_val
    o_gd = o_gs + D
    assert o_gd + D == w_in.shape[2]
    assert 2 * n_heads <= LANES
    tm = min(MM_TM, L)
    mm = functools.partial(_matmul, tm=tm, tn=MM_TN)
    w_in_t = jnp.swapaxes(w_in, 1, 2)

    xs = x.reshape(L, D)
    h = _prenorm(xs, mix_pre_w[0])
    tables = _ssm_tables(ssm_a_re, ssm_a_im, ssm_log_dt, ssm_b_re, ssm_b_im, ssm_c_re, ssm_c_im, ssm_d)
    for i in range(depth):
        u = mm(h, [(w_in_t, i, o_u)], n=ssm_w, out_dtype=F32, tk=D, nt=True, name="in_proj_u")
        conv_store = functools.partial(_conv_silu_norm_store, tm=tm, dk=dk, n_q=gdn_key // MM_TN,
                                       n_qk=2 * gdn_key // MM_TN, q_scale=dk ** -0.5)
        qkv = mm(h, [(w_in_t, i, o_qkv)], n=conv_ch, out_dtype=BF16, tk=D, nt=True, name="in_proj_qkv",
                 epi=[(conv_w[i].astype(F32), (conv_w.shape[1], MM_TN), lambda j, k, r: (0, j))],
                 post=(conv_store, _conv_init),
                 post_scratch=[pltpu.VMEM((CONV_HALO + tm, MM_TN), F32)])
        ab = _matmul(h, [(w_in_t, i, o_a)], n=LANES, out_dtype=F32, tm=tm, tn=LANES, tk=D, nt=True,
                     name="in_proj_ab")
        zg = mm(h, [(w_in_t, i, o_z)], n=gdn_val + 2 * D, out_dtype=BF16, tk=D, nt=True,
                name="in_proj_zg")

        y = _ssm(u, tables, i)
        y_ssm = mm(y, [(w_glu, i, 0), (w_glu, i, D)], n=D, out_dtype=BF16, tk=ssm_w,
                   name="glu", epilogue=_glu_epilogue)

        o = _gdn(qkv, ab, zg, gdn_a_log[i], gdn_dt_bias[i], gdn_norm_w[i], n_heads=n_heads, dk=dk, dv=dv)
        merged = mm(o, [(w_gdn_out, i, 0)], n=D, out_dtype=BF16, tk=gdn_val,
                    name="gdn_out_merge", epilogue=_merge_epilogue,
                    epi=[(zg, gdn_val // MM_TN), (zg, (gdn_val + D) // MM_TN), (y_ssm, 0)])
        hm = mm(merged, [(w_out, i, 0)], n=D, out_dtype=BF16, tk=D, name="out_proj")
        xs, hn = _resnorm(hm, xs, mix_post_w[i], ffn_pre_w[i])

        f1 = mm(hn, [(w_ff1, i, 0)], n=d_ff, out_dtype=BF16, tk=D, name="ff1", epilogue=_relu2_epilogue)
        f2 = _matmul(f1, [(w_ff2, i, 0)], n=D, out_dtype=BF16, tm=tm, tn=MM_TN_FF2, tk=MM_TK_FF2, name="ff2")
        xs, h = _resnorm(f2, xs, ffn_post_w[i], mix_pre_w[i + 1] if i + 1 < depth else None)
    return xs.reshape(B, L, D)
```

```python
import functools
import math

import jax
import jax.numpy as jnp
import numpy as np
from jax import lax
from jax.experimental import pallas as pl
from jax.experimental.pallas import tpu as pltpu

F32 = jnp.float32
BF16 = jnp.bfloat16
EPS = 1e-6

V7X_VMEM_LIMIT_BYTES = 56 * 1024 * 1024
LANES = 128
SSM_FOLD = 8
GDN_CHUNK = 64
GDN_CHUNKS_PER_STEP = 2
CONV_HALO = 8
EPI_ROWS = 16
MM_TM = 1024
MM_TN = 1024
MM_TN_FF2 = 512
MM_TK_FF2 = 4096


def _cparams(sem):
    return pltpu.CompilerParams(dimension_semantics=sem, vmem_limit_bytes=V7X_VMEM_LIMIT_BYTES)


def _rms(xf, w):
    return xf * lax.rsqrt(jnp.mean(xf * xf, axis=-1, keepdims=True) + EPS) * w


def _sigmoid(x):
    return 1.0 / (1.0 + jnp.exp(-x))


def _prenorm_kernel(x_ref, w_ref, o_ref):
    o_ref[...] = _rms(x_ref[...], w_ref[...]).astype(o_ref.dtype)


def _prenorm(x, w, *, tr=256):
    L, D = x.shape
    return pl.pallas_call(
        _prenorm_kernel,
        grid=(L // tr,),
        in_specs=[pl.BlockSpec((tr, D), lambda i: (i, 0)), pl.BlockSpec((1, D), lambda i: (0, 0))],
        out_specs=pl.BlockSpec((tr, D), lambda i: (i, 0)),
        out_shape=jax.ShapeDtypeStruct((L, D), BF16),
        compiler_params=_cparams(("parallel",)),
        name="prenorm",
    )(x, w.reshape(1, D))


def _resnorm_kernel(h_ref, x_ref, wpost_ref, wpre_ref, xo_ref, ho_ref):
    xn = x_ref[...] + _rms(h_ref[...].astype(F32), wpost_ref[...])
    xo_ref[...] = xn
    ho_ref[...] = _rms(xn, wpre_ref[...]).astype(ho_ref.dtype)


def _resnorm_last_kernel(h_ref, x_ref, wpost_ref, xo_ref):
    xo_ref[...] = x_ref[...] + _rms(h_ref[...].astype(F32), wpost_ref[...])


def _resnorm(h, x, w_post, w_pre, *, tr=256):
    L, D = x.shape
    row = pl.BlockSpec((tr, D), lambda i: (i, 0))
    vec = pl.BlockSpec((1, D), lambda i: (0, 0))
    if w_pre is None:
        return pl.pallas_call(
            _resnorm_last_kernel, grid=(L // tr,), in_specs=[row, row, vec], out_specs=row,
            out_shape=jax.ShapeDtypeStruct((L, D), F32),
            compiler_params=_cparams(("parallel",)), name="resnorm_last",
        )(h, x, w_post.reshape(1, D)), None
    return pl.pallas_call(
        _resnorm_kernel, grid=(L // tr,), in_specs=[row, row, vec, vec], out_specs=[row, row],
        out_shape=[jax.ShapeDtypeStruct((L, D), F32), jax.ShapeDtypeStruct((L, D), BF16)],
        compiler_params=_cparams(("parallel",)), name="resnorm",
    )(h, x, w_post.reshape(1, D), w_pre.reshape(1, D))


def _mm_kernel(*refs, n_w, n_epi, nk, n_chunk, tm, tn, tk, rc, nt, w_index, epilogue, post):
    a_ref = refs[0]
    w_hbm = refs[1:1 + n_w]
    epi_refs = refs[1 + n_w:1 + n_w + n_epi]
    o_ref = refs[1 + n_w + n_epi]
    scratch = refs[2 + n_w + n_epi:]
    wb_even = scratch[:n_w]
    wb_odd = scratch[n_w:2 * n_w]
    stage_refs = scratch[2 * n_w:3 * n_w]
    sem = scratch[3 * n_w]
    n_acc = n_w if nk > 1 else 0
    acc_refs = scratch[3 * n_w + 1:3 * n_w + 1 + n_acc]
    post_refs = scratch[3 * n_w + 1 + n_acc:]
    j = pl.program_id(0)
    k = pl.program_id(1)
    i = pl.program_id(2)
    b = j * nk + k
    n_blocks = pl.num_programs(0) * nk

    def chunk_copy(w, blk, c, sslot):
        jj, kk = (blk, 0) if nk == 1 else (blk // nk, lax.rem(blk, nk))
        layer, off = w_index[w]
        if nt:
            src = w_hbm[w].at[layer, pl.ds(off + jj * tn + c * rc, rc), pl.ds(kk * tk, tk)]
        else:
            src = w_hbm[w].at[layer, pl.ds(kk * tk + c * rc, rc), pl.ds(off + jj * tn, tn)]
        return pltpu.make_async_copy(src, stage_refs[w].at[sslot], sem.at[w, sslot])

    n_i = n_chunk
    t = b * n_i + i
    tslot = lax.rem(t, 2)

    def step_chunk(tt):
        return jnp.minimum(tt // n_i + 1, n_blocks - 1), jnp.minimum(lax.rem(tt, n_i), n_chunk - 1)

    def round_chunk(w, wb, c, sslot):
        wb[w][pl.ds(pl.multiple_of(c * rc, rc), rc), :] = stage_refs[w][sslot].astype(BF16)

    @pl.when(t == 0)
    def _():
        if post is not None:
            post[1](post_refs)
        for acc in acc_refs:
            acc[...] = jnp.zeros(acc.shape, F32)
        steps = (tn if nt else tk) // rc
        for w in range(n_w):
            chunk_copy(w, 0, 0, 0).start()
            for c in range(steps):
                if c + 1 < steps:
                    chunk_copy(w, 0, c + 1, (c + 1) % 2).start()
                chunk_copy(w, 0, c, c % 2).wait()
                round_chunk(w, wb_even, c, c % 2)
        for w in range(n_w):
            chunk_copy(w, *step_chunk(0), 0).start()
            chunk_copy(w, *step_chunk(1), 1).start()

    def finish(accs):
        if post is not None:
            post[0](o_ref, accs, epi_refs, post_refs, j, i)
        else:
            for r in range(0, tm, EPI_ROWS):
                rows_ = slice(r, r + EPI_ROWS)
                o_ref[rows_, :] = epilogue(*[x[rows_, :] for x in accs],
                                           *[e[rows_, :] for e in epi_refs]).astype(o_ref.dtype)

    def body(wb_use, wb_fill):
        blk, c = step_chunk(t)
        for w in range(n_w):
            chunk_copy(w, blk, c, tslot).wait()
            round_chunk(w, wb_fill, c, tslot)
        blk, c = step_chunk(t + 2)
        for w in range(n_w):
            chunk_copy(w, blk, c, tslot).start()
        a = a_ref[...].astype(BF16)
        if nt:
            parts = [_dot_nt(a, wb[...]) for wb in wb_use]
        else:
            parts = [jnp.dot(a, wb[...], preferred_element_type=F32) for wb in wb_use]
        if nk == 1:
            finish(parts)
            return
        rows = pl.ds(pl.multiple_of(i * tm, tm), tm)
        sums = [jnp.where(k > 0, acc[rows, :], 0.0) + part for acc, part in zip(acc_refs, parts)]
        for acc, total in zip(acc_refs, sums):
            acc[rows, :] = total
        finish(sums)

    even = lax.rem(b, 2) == 0
    pl.when(even)(lambda: body(wb_even, wb_odd))
    pl.when(jnp.logical_not(even))(lambda: body(wb_odd, wb_even))

    @pl.when(t == n_blocks * n_i - 1)
    def _():
        for ahead in (1, 2):
            blk, c = step_chunk(t + ahead)
            for w in range(n_w):
                chunk_copy(w, blk, c, lax.rem(t + ahead, 2)).wait()


def _matmul(a, ws, *, n, out_dtype, tm, tn, tk, name, nt=False, epilogue=None, epi=(), post=None, post_scratch=()):
    M, K = a.shape
    nk = K // tk
    n_i = M // tm
    assert M % tm == 0 and n % tn == 0 and K % tk == 0
    rc = (tn if nt else tk) // n_i
    assert rc * n_i == (tn if nt else tk) and rc % 16 == 0
    if epilogue is None:
        epilogue = lambda acc: acc
    in_specs = [pl.BlockSpec((tm, tk), lambda j, k, i: (i, k))]
    args = [a]
    for w, _, _ in ws:
        in_specs.append(pl.BlockSpec(memory_space=pl.ANY))
        args.append(w)
    for e, *spec in epi:
        if len(spec) == 1:
            in_specs.append(pl.BlockSpec((tm, tn), lambda j, k, i, off=spec[0]: (i, off + j)))
        else:
            in_specs.append(pl.BlockSpec(*spec))
        args.append(e)
    wshape = (tn, tk) if nt else (tk, tn)
    scratch = [pltpu.VMEM(wshape, BF16) for _ in ws] * 2
    scratch += [pltpu.VMEM((2, rc, wshape[1]), F32) for _ in ws]
    scratch += [pltpu.SemaphoreType.DMA((len(ws), 2))]
    if nk > 1:
        scratch += [pltpu.VMEM((M, tn), F32) for _ in ws]
        out_map = lambda j, k, i: (jnp.where(k == nk - 1, i, 0), j)
    else:
        out_map = lambda j, k, i: (i, j)
    scratch += list(post_scratch)
    kern = functools.partial(_mm_kernel, n_w=len(ws), n_epi=len(epi), nk=nk, n_chunk=n_i, tm=tm, tn=tn, tk=tk, rc=rc,
                             nt=nt,
                             w_index=tuple((layer, off) for _, layer, off in ws), epilogue=epilogue, post=post)
    return pl.pallas_call(
        kern,
        grid=(n // tn, nk, n_i),
        in_specs=in_specs,
        out_specs=pl.BlockSpec((tm, tn), out_map),
        out_shape=jax.ShapeDtypeStruct((M, n), out_dtype),
        scratch_shapes=scratch,
        compiler_params=_cparams(("arbitrary", "arbitrary", "arbitrary")),
        name=name,
    )(*args)


def _glu_epilogue(acc_a, acc_b):
    return acc_a * _sigmoid(acc_b)


def _merge_epilogue(acc, gate_s, gate_d, y_ssm):
    return (_sigmoid(gate_s.astype(F32)) * y_ssm.astype(F32)
            + _sigmoid(gate_d.astype(F32)) * acc)


def _relu2_epilogue(acc):
    r = jnp.maximum(acc, 0.0)
    return r * r


def _dot_nt(a, b):
    return lax.dot_general(a, b, (((1,), (1,)), ((), ())), preferred_element_type=F32)


def _dot_tn(a, b):
    return lax.dot_general(a, b, (((0,), (0,)), ((), ())), preferred_element_type=F32)


def _ssm_tables(a_re, a_im, log_dt, b_re, b_im, c_re, c_im, d_skip):
    T = SSM_FOLD
    depth, G, N = a_re.shape
    H = b_re.shape[-1]
    gps = LANES // H
    n_slab = G // gps
    rows = T * gps * H
    dup = lambda t: jnp.concatenate([t, t], axis=-1)
    dt = jnp.exp(log_dt.astype(F32))[..., None]
    ar = dup(a_re.astype(F32))
    ai = dup(a_im.astype(F32))
    mag = jnp.exp(ar * dt)
    abar_r = mag * jnp.cos(ai * dt)
    abar_i = mag * jnp.sin(ai * dt)
    den = ar * ar + ai * ai
    fr = (((abar_r - 1.0) * ar + abar_i * ai) / den)[:, :, None, :]
    fi = ((abar_i * ar - (abar_r - 1.0) * ai) / den)[:, :, None, :]
    br = dup(jnp.swapaxes(b_re.astype(F32), -1, -2))
    bi = dup(jnp.swapaxes(b_im.astype(F32), -1, -2))
    bbar_r = fr * br - fi * bi
    bbar_i = fr * bi + fi * br
    p = jnp.arange(T + 1, dtype=F32)[None, :, None, None]
    pmag = jnp.exp(p * (ar * dt)[:, None])
    pow_r = (pmag * jnp.cos(p * (ai * dt)[:, None]))[:, :, :, None, :]
    pow_i = (pmag * jnp.sin(p * (ai * dt)[:, None]))[:, :, :, None, :]
    pow_r, pow_i, bbar_r, bbar_i = lax.optimization_barrier((pow_r, pow_i, bbar_r, bbar_i))
    cr = dup(c_re.astype(F32))[:, None]
    ci = dup(c_im.astype(F32))[:, None]

    def compact(tr, ti):
        t = jnp.stack([tr, ti], axis=1).astype(BF16).reshape(depth, 2, T, n_slab, gps * H, 2 * N)
        return t.transpose(0, 3, 1, 2, 4, 5).reshape(depth, n_slab, 2, rows, 2 * N)

    pw_r = pow_r[:, T - 1::-1]
    pw_i = pow_i[:, T - 1::-1]
    pc = compact(pw_r * bbar_r[:, None] - pw_i * bbar_i[:, None], pw_r * bbar_i[:, None] + pw_i * bbar_r[:, None])

    ca_r = cr * pow_r - ci * pow_i
    ca_i = cr * pow_i + ci * pow_r
    qc = compact(ca_r[:, 1:], -ca_i[:, 1:])

    kb = (jnp.einsum("dlghn,dgkn->dgklh", ca_r[:, :T, ..., :N], bbar_r[..., :N])
          - jnp.einsum("dlghn,dgkn->dgklh", ca_i[:, :T, ..., :N], bbar_i[..., :N]))
    kb = kb.reshape(depth, n_slab, gps * H, T * H).astype(BF16)
    src = np.arange(T * H)
    dst = np.arange(rows)
    spread = ((src[:, None] // H == dst[None, :] // (gps * H)) & (src[:, None] % H == dst[None, :] % H))
    spread = jnp.asarray(spread, BF16)

    a_pow = jnp.stack([pow_r[:, T, ..., :N].reshape(depth, n_slab, gps * N),
                       pow_i[:, T, ..., :N].reshape(depth, n_slab, gps * N)], axis=2)
    d_tab = jnp.tile(d_skip.astype(F32).reshape(depth, n_slab, 1, LANES), (1, 1, 1, T))
    return pc, qc, kb, spread, a_pow, d_tab


def _ssm_kernel(u_ref, p_ref, q_ref, k_ref, spread_ref, apow_ref, d_ref, o_ref, s_ref, xp_ref, *, group):
    T = SSM_FOLD
    nc = u_ref.shape[0] // T
    ns = s_ref.shape[1]
    half = ns // 2
    fw = T * LANES
    n_state = p_ref.shape[2] // 2
    gps = LANES // group
    row_g = (lax.broadcasted_iota(jnp.int32, (fw, LANES), 0) // group) % gps
    lane = lax.broadcasted_iota(jnp.int32, (fw, LANES), 1)

    def state_table(ref):
        tiles = []
        for c in range(2):
            t = ref[c]
            for v in range(half // LANES):
                tiles.append(jnp.where(row_g == v * (LANES // n_state) + lane // n_state, t, jnp.zeros_like(t)))
        return jnp.concatenate(tiles, axis=1)

    p_tab = state_table(p_ref)
    qt_tab = state_table(q_ref)
    kb = k_ref[...].astype(F32)
    klane = lax.broadcasted_iota(jnp.int32, kb.shape, 1)
    kc = jnp.concatenate(
        [kb] + [jnp.where(klane >= group * i, pltpu.roll(kb, group * i, axis=1), 0.0) for i in range(1, T)],
        axis=0).astype(BF16)
    m_rep = jnp.dot(kc, spread_ref[...], preferred_element_type=F32)
    same_g = ((lax.broadcasted_iota(jnp.int32, (fw, fw), 0) // group) % gps
              == (lax.broadcasted_iota(jnp.int32, (fw, fw), 1) // group) % gps)
    m_tab = jnp.where(same_g, m_rep, 0.0).astype(BF16)
    uf32 = jnp.concatenate([u_ref[pl.ds(i, nc, stride=T), :] for i in range(T)], axis=1)
    uf = uf32.astype(BF16)
    s_ref[...] = jnp.dot(uf, p_tab, preferred_element_type=F32)
    ar = apow_ref[0:1, :]
    ai = apow_ref[1:2, :]

    def step(c, carry):
        xr, xi = carry
        xp_ref[pl.ds(c, 1), 0:half] = xr
        xp_ref[pl.ds(c, 1), half:2 * half] = xi
        sr = s_ref[pl.ds(c, 1), 0:half]
        si = s_ref[pl.ds(c, 1), half:2 * half]
        return ar * xr - ai * xi + sr, ar * xi + ai * xr + si

    zero = jnp.zeros((1, half), F32)
    lax.fori_loop(0, nc, step, (zero, zero), unroll=8)
    y = (jnp.dot(uf, m_tab, preferred_element_type=F32)
         + _dot_nt(xp_ref[...].astype(BF16), qt_tab)
         + d_ref[...] * uf32)
    g = jax.nn.gelu(y, approximate=True)
    for j in range(T):
        o_ref[pl.ds(j, nc, stride=T), :] = g[:, j * LANES:(j + 1) * LANES]


def _ssm(u, tables, layer):
    pc, qc, kb, spread, a_pow, d_tab = tables
    L, W = u.shape
    n_slab = W // LANES
    nc = L // SSM_FOLD
    fw = SSM_FOLD * LANES
    ns = 2 * a_pow.shape[3]
    group = LANES * pc.shape[4] // ns
    tab = lambda *shape: pl.BlockSpec((None, None) + shape, lambda s: (layer, s) + (0,) * len(shape))
    return pl.pallas_call(
        functools.partial(_ssm_kernel, group=group),
        grid=(n_slab,),
        in_specs=[pl.BlockSpec((L, LANES), lambda s: (0, s)),
                  tab(2, fw, pc.shape[4]), tab(2, fw, qc.shape[4]), tab(*kb.shape[2:]),
                  pl.BlockSpec(spread.shape, lambda s: (0, 0)),
                  tab(2, ns // 2), tab(1, fw)],
        out_specs=pl.BlockSpec((L, LANES), lambda s: (0, s)),
        out_shape=jax.ShapeDtypeStruct((L, W), F32),
        scratch_shapes=[pltpu.VMEM((nc, ns), F32), pltpu.VMEM((nc, ns), F32)],
        compiler_params=_cparams(("parallel",)),
        name="s5_ssm",
    )(u, pc, qc, kb, spread, a_pow, d_tab)


def _gdn_kernel(qkv_ref, ab_ref, z_ref, alog_ref, dtb_ref, normw_ref, o_ref, s_ref, *, n_heads, dk, dv):
    C = GDN_CHUNK
    c = pl.program_id(0)

    @pl.when(c == 0)
    def _():
        s_ref[...] = jnp.zeros(s_ref.shape, F32)

    row = lax.broadcasted_iota(jnp.int32, (C, 2 * C), 0)
    col = lax.broadcasted_iota(jnp.int32, (C, 2 * C), 1)
    colm = jnp.where(col >= C, col - C, col)
    causal2 = colm <= row
    strict2 = colm < row
    low_half = col < C
    eye2 = jnp.where(colm == row, 1.0, 0.0)
    tri = jnp.where(lax.broadcasted_iota(jnp.int32, (C, C), 1) <= lax.broadcasted_iota(jnp.int32, (C, C), 0),
                    1.0, 0.0)

    ab = ab_ref[...]
    sp_in = ab + dtb_ref[...]
    softplus = jnp.maximum(sp_in, 0.0) + jnp.log(1.0 + jnp.exp(-jnp.abs(sp_in)))
    g = -jnp.exp(alog_ref[...]) * softplus
    beta_all = _sigmoid(ab)

    heads = range(n_heads)
    n_sub = ab_ref.shape[0] // C
    zero_cc = jnp.zeros((C, 2 * C), F32)
    dot = functools.partial(jnp.dot, preferred_element_type=F32)
    st = []
    for sub, h in [(sub, h) for sub in range(n_sub) for h in heads]:
        rs = slice(sub * C, (sub + 1) * C)
        if h == 0:
            beta = beta_all[rs]
            gc = jnp.dot(tri, g[rs], preferred_element_type=F32, precision=lax.Precision.HIGHEST)
            gct = jnp.transpose(jnp.concatenate([gc, gc], axis=0))
        qn = qkv_ref[rs, h * dk:(h + 1) * dk].astype(F32)
        kn = qkv_ref[rs, (n_heads + h) * dk:(n_heads + h + 1) * dk].astype(F32)
        vh = qkv_ref[rs, 2 * n_heads * dk + h * dv:2 * n_heads * dk + (h + 1) * dv].astype(F32)
        gcol = gc[:, h:h + 1]
        grow2 = gct[h:h + 1, :]
        bcol = beta[:, n_heads + h:n_heads + h + 1]
        glast = gc[C - 1:C, h:h + 1]
        egc = jnp.exp(gcol)
        kb = kn * bcol
        r = jnp.concatenate([vh * bcol, kb * egc], axis=1)
        st.append(dict(
            decay2=jnp.exp(jnp.where(causal2, gcol - grow2, -jnp.inf)),
            k2=jnp.concatenate([kn, kn], axis=0).astype(BF16),
            kbq=jnp.concatenate([kb, qn], axis=0).astype(BF16),
            r2=jnp.concatenate([r, jnp.zeros_like(r)], axis=0).astype(BF16),
            qg=qn * egc,
            kd=(kn * jnp.exp(glast - gcol)).astype(BF16),
            eg=jnp.exp(glast)))
    top = lambda x: jnp.concatenate([x, zero_cc], axis=0).astype(BF16)
    sc = [_dot_nt(t["kbq"], t["k2"]) for t in st]
    n2 = [jnp.where(strict2, s_[0:C] * t["decay2"], 0.0) for s_, t in zip(sc, st)]
    qk2 = [jnp.where(causal2 & low_half, s_[C:2 * C] * t["decay2"], 0.0).astype(BF16) for s_, t in zip(sc, st)]
    valid = low_half & strict2
    blk = 2
    w = [jnp.where(low_half, eye2, 0.0) - jnp.where(valid & (row // blk == colm // blk), n, 0.0) for n in n2]
    while blk < C:
        off = valid & (row // (2 * blk) == colm // (2 * blk)) & (row // blk > colm // blk)
        y = [dot(jnp.where(off, n, 0.0).astype(BF16), top(x)) for n, x in zip(n2, w)]
        w = [x - dot(x.astype(BF16), top(y_)) for x, y_ in zip(w, y)]
        blk *= 2
    uw = [dot(x.astype(BF16), t["r2"]) for x, t in zip(w, st)]
    state = [s_ref[h] for h in heads]
    for sub in range(n_sub):
        rs = slice(sub * C, (sub + 1) * C)
        part = slice(sub * n_heads, (sub + 1) * n_heads)
        ws = [dot(jnp.concatenate([x[:, dv:dv + dk], t["qg"]], axis=0).astype(BF16), s_.astype(BF16))
              for x, t, s_ in zip(uw[part], st[part], state)]
        vn = [x[:, 0:dv] - y[0:C] for x, y in zip(uw[part], ws)]
        o = [y[C:2 * C] + dot(q_, jnp.concatenate([v_, v_], axis=0).astype(BF16))
             for y, q_, v_ in zip(ws, qk2[part], vn)]
        state = [s_ * t["eg"] + _dot_tn(t["kd"], v_.astype(BF16)) for s_, t, v_ in zip(state, st[part], vn)]
        for h in heads:
            zh = z_ref[rs, h * dv:(h + 1) * dv].astype(F32)
            on = _rms(o[h], normw_ref[...]) * (zh * _sigmoid(zh))
            o_ref[rs, h * dv:(h + 1) * dv] = on.astype(o_ref.dtype)
    for h in heads:
        s_ref[h] = state[h]


def _conv_silu_norm_store(o_ref, accs, epi_refs, post_refs, j, i, *, tm, dk, n_q, n_qk, q_scale):
    (cw_ref,) = epi_refs
    (buf,) = post_refs
    kw = cw_ref.shape[0]
    halo = CONV_HALO
    buf[0:halo, :] = jnp.where(i > 0, buf[tm:tm + halo, :], 0.0)
    buf[halo:, :] = accs[0]
    scale = jnp.where(j < n_q, q_scale, 1.0)
    for s in range(buf.shape[1] // dk):
        lanes = slice(s * dk, (s + 1) * dk)
        taps = [cw_ref[t:t + 1, lanes] for t in range(kw)]
        for r in range(0, tm, EPI_ROWS):
            ext = buf[r:r + halo + EPI_ROWS, lanes]
            y = ext[halo:, :] * taps[kw - 1]
            for d in range(1, kw):
                y = y + pltpu.roll(ext, d, axis=0)[halo:, :] * taps[kw - 1 - d]
            x = y * _sigmoid(y)
            inv = lax.rsqrt(jnp.sum(x * x, axis=-1, keepdims=True) + EPS) * scale
            inv = jnp.where(j < n_qk, inv, 1.0)
            o_ref[r:r + EPI_ROWS, lanes] = (x * inv).astype(o_ref.dtype)


def _conv_init(post_refs):
    (buf,) = post_refs
    buf[...] = jnp.zeros(buf.shape, F32)


def _gdn(qkv, ab, zg, a_log, dt_bias, norm_w, *, n_heads, dk, dv):
    L, cw = qkv.shape
    C = GDN_CHUNK * GDN_CHUNKS_PER_STEP
    pad = lambda v: jnp.pad(v.astype(F32), (0, LANES - v.shape[0])).reshape(1, LANES)
    kern = functools.partial(_gdn_kernel, n_heads=n_heads, dk=dk, dv=dv)
    return pl.pallas_call(
        kern,
        grid=(L // C,),
        in_specs=[pl.BlockSpec((C, cw), lambda c: (c, 0)),
                  pl.BlockSpec((C, LANES), lambda c: (c, 0)),
                  pl.BlockSpec((C, n_heads * dv), lambda c: (c, 0)),
                  pl.BlockSpec((1, LANES), lambda c: (0, 0)),
                  pl.BlockSpec((1, LANES), lambda c: (0, 0)),
                  pl.BlockSpec((1, dv), lambda c: (0, 0))],
        out_specs=pl.BlockSpec((C, n_heads * dv), lambda c: (c, 0)),
        out_shape=jax.ShapeDtypeStruct((L, n_heads * dv), BF16),
        scratch_shapes=[pltpu.VMEM((n_heads, dk, dv), F32)],
        compiler_params=_cparams(("arbitrary",)),
        name="gated_deltanet",
    )(qkv, ab, zg, pad(a_log), pad(dt_bias), norm_w.astype(F32).reshape(1, dv))


def kernel(x, w_in, ssm_a_re, ssm_a_im, ssm_log_dt, ssm_b_re, ssm_b_im, ssm_c_re, ssm_c_im, ssm_d, w_glu, conv_w, gdn_a_log, gdn_dt_bias, gdn_norm_w, w_gdn_out, w_out, mix_pre_w, mix_post_w, ffn_pre_w, ffn_post_w, w_ff1, w_ff2):
    depth = w_in.shape[0]
    B, L, D = x.shape
    assert B == 1
    n_heads = gdn_a_log.shape[1]
    dv = gdn_norm_w.shape[1]
    gdn_val = w_gdn_out.shape[1]
    assert gdn_val == n_heads * dv
    conv_ch = conv_w.shape[2]
    gdn_key = (conv_ch - gdn_val) // 2
    dk = gdn_key // n_heads
    ssm_w = ssm_d.shape[1]
    d_ff = w_ff1.shape[2]
    o_u = 0
    o_qkv = o_u + ssm_w
    o_a = o_qkv + conv_ch
    o_b = o_a + n_heads
    o_z = o_b + n_heads
    o_gs = o_z + gdn_val
    o_gd = o_gs + D
    assert o_gd + D == w_in.shape[2]
    assert 2 * n_heads <= LANES
    tm = min(MM_TM, L)
    mm = functools.partial(_matmul, tm=tm, tn=MM_TN)
    w_in_t = jnp.swapaxes(w_in, 1, 2)

    xs = x.reshape(L, D)
    h = _prenorm(xs, mix_pre_w[0])
    tables = _ssm_tables(ssm_a_re, ssm_a_im, ssm_log_dt, ssm_b_re, ssm_b_im, ssm_c_re, ssm_c_im, ssm_d)
    for i in range(depth):
        u = mm(h, [(w_in_t, i, o_u)], n=ssm_w, out_dtype=F32, tk=D, nt=True, name="in_proj_u")
        conv_store = functools.partial(_conv_silu_norm_store, tm=tm, dk=dk, n_q=gdn_key // MM_TN,
                                       n_qk=2 * gdn_key // MM_TN, q_scale=dk ** -0.5)
        qkv = mm(h, [(w_in_t, i, o_qkv)], n=conv_ch, out_dtype=BF16, tk=D, nt=True, name="in_proj_qkv",
                 epi=[(conv_w[i].astype(F32), (conv_w.shape[1], MM_TN), lambda j, k, r: (0, j))],
                 post=(conv_store, _conv_init),
                 post_scratch=[pltpu.VMEM((CONV_HALO + tm, MM_TN), F32)])
        ab = _matmul(h, [(w_in_t, i, o_a)], n=LANES, out_dtype=F32, tm=tm, tn=LANES, tk=D, nt=True,
                     name="in_proj_ab")
        zg = mm(h, [(w_in_t, i, o_z)], n=gdn_val + 2 * D, out_dtype=BF16, tk=D, nt=True,
                name="in_proj_zg")

        y = _ssm(u, tables, i)
        y_ssm = mm(y, [(w_glu, i, 0), (w_glu, i, D)], n=D, out_dtype=BF16, tk=ssm_w,
                   name="glu", epilogue=_glu_epilogue)

        o = _gdn(qkv, ab, zg, gdn_a_log[i], gdn_dt_bias[i], gdn_norm_w[i], n_heads=n_heads, dk=dk, dv=dv)
        merged = mm(o, [(w_gdn_out, i, 0)], n=D, out_dtype=BF16, tk=gdn_val,
                    name="gdn_out_merge", epilogue=_merge_epilogue,
                    epi=[(zg, gdn_val // MM_TN), (zg, (gdn_val + D) // MM_TN), (y_ssm, 0)])
        hm = mm(merged, [(w_out, i, 0)], n=D, out_dtype=BF16, tk=D, name="out_proj")
        xs, hn = _resnorm(hm, xs, mix_post_w[i], ffn_pre_w[i])

        f1 = mm(hn, [(w_ff1, i, 0)], n=d_ff, out_dtype=BF16, tk=D, name="ff1", epilogue=_relu2_epilogue)
        f2 = _matmul(f1, [(w_ff2, i, 0)], n=D, out_dtype=BF16, tm=tm, tn=MM_TN_FF2, tk=MM_TK_FF2, name="ff2")
        xs, h = _resnorm(f2, xs, ffn_post_w[i], mix_pre_w[i + 1] if i + 1 < depth else None)
    return xs.reshape(B, L, D)
```
